```python
import math
import jax
import jax.numpy as jnp
from jax import lax
import numpy as np

D_MODEL = 2048
BATCH = 1
SEQ = 16384
DEPTH = 1
DEC_BATCH = 2
DEC_SEQ = 16384
PAST_LEN = 128

HEAD_DIM = 128
N_QK_HEADS = 16
N_V_HEADS = 32
QK_WIDTH = N_QK_HEADS * HEAD_DIM
V_WIDTH = N_V_HEADS * HEAD_DIM
SHORT_CONV = 5
CHUNK = 64
N_DIR = 2
CONF_WIDTH = D_MODEL
CONF_KERNEL = 31
D_FF = 5632
FFN_CONV = 3
RMS_EPS = 1e-6
LN_EPS = 1e-5
L2_EPS = 1e-6
IN_COLS = 2 * QK_WIDTH + 2 * V_WIDTH + 2 * N_DIR * N_V_HEADS + 2 * CONF_WIDTH + 2 * D_MODEL

kernel_name = 'bidir_gated_deltanet_conformer_encoder'

F32 = jnp.float32


def rms_norm(x, w):
    xf = x.astype(F32)
    y = xf * lax.rsqrt(jnp.mean(xf * xf, axis=-1, keepdims=True) + RMS_EPS)
    return (y * w.astype(F32)).astype(x.dtype)


def layer_norm(x, w, b):
    xf = x.astype(F32)
    mu = jnp.mean(xf, axis=-1, keepdims=True)
    xc = xf - mu
    var = jnp.mean(xc * xc, axis=-1, keepdims=True)
    return (xc * lax.rsqrt(var + LN_EPS) * w.astype(F32) + b.astype(F32)).astype(x.dtype)


def l2_normalize(x):
    return x * lax.rsqrt(jnp.sum(x * x, axis=-1, keepdims=True) + L2_EPS)


def depthwise_conv(x, w):
    pad = (w.shape[0] - 1) // 2
    return lax.conv_general_dilated(
        x, w[:, None, :].astype(x.dtype), (1,), [(pad, pad)],
        dimension_numbers=('NWC', 'WIO', 'NWC'), feature_group_count=x.shape[-1])


def split_in_proj(proj):
    sizes = (QK_WIDTH, QK_WIDTH, V_WIDTH, V_WIDTH, N_DIR * N_V_HEADS, N_DIR * N_V_HEADS,
             2 * CONF_WIDTH, 2 * D_MODEL)
    return jnp.split(proj, np.cumsum(sizes)[:-1].tolist(), axis=-1)


def chunk_gated_delta_rule(q, k, v, g, beta):
    bsz, seq, nh, dk = q.shape
    dv = v.shape[-1]
    n_chunks = seq // CHUNK

    def to_chunks(t):
        t = t.reshape((bsz, n_chunks, CHUNK, nh) + t.shape[3:])
        return jnp.moveaxis(t, 3, 1)

    q, k, v, g, beta = (to_chunks(t) for t in (q, k, v, g, beta))
    q = q * (dk ** -0.5)
    g = jnp.cumsum(g, axis=-1)
    idx = jnp.arange(CHUNK)
    incl = idx[:, None] >= idx[None, :]
    strict = idx[:, None] > idx[None, :]
    diff = g[..., :, None] - g[..., None, :]
    decay = jnp.where(incl, jnp.exp(jnp.where(incl, diff, 0.0)), 0.0)

    k_beta = k * beta[..., None]
    v_beta = v * beta[..., None]
    lmat = jnp.where(strict, jnp.einsum('bhncd,bhnsd->bhncs', k_beta, k) * decay, 0.0)
    rhs = jnp.concatenate([v_beta, k_beta * jnp.exp(g)[..., None]], axis=-1)
    sol = lax.linalg.triangular_solve(lmat, rhs, left_side=True, lower=True, unit_diagonal=True)
    u, w = sol[..., :dv], sol[..., dv:]

    qk_intra = jnp.where(incl, jnp.einsum('bhncd,bhnsd->bhncs', q, k) * decay, 0.0)
    g_last = g[..., -1]
    q_dec = q * jnp.exp(g)[..., None]
    k_dec = k * jnp.exp(g_last[..., None] - g)[..., None]

    def step(state, inp):
        qd, kd, wc, uc, ac, gl = inp
        v_new = uc - jnp.einsum('bhck,bhkv->bhcv', wc, state)
        out = jnp.einsum('bhck,bhkv->bhcv', qd, state) + jnp.einsum('bhcs,bhsv->bhcv', ac, v_new)
        state = state * jnp.exp(gl)[..., None, None] + jnp.einsum('bhck,bhcv->bhkv', kd, v_new)
        return state, out

    xs = tuple(jnp.moveaxis(t, 2, 0) for t in (q_dec, k_dec, w, u, qk_intra, g_last))
    state0 = jnp.zeros((bsz, nh, dk, dv), F32)
    _, out = lax.scan(step, state0, xs)
    out = jnp.moveaxis(jnp.moveaxis(out, 0, 2), 1, 3)
    return out.reshape(bsz, seq, nh, dv)


def gated_deltanet_bidir(q, k, v, z, a, b, conv_w, a_log, dt_bias, norm_w, w_out):
    bsz, seq, _ = q.shape
    qkv = jax.nn.silu(depthwise_conv(jnp.concatenate([q, k, v], axis=-1), conv_w))
    q, k, v = jnp.split(qkv, [QK_WIDTH, 2 * QK_WIDTH], axis=-1)
    rep = N_V_HEADS // N_QK_HEADS
    q = jnp.repeat(l2_normalize(q.astype(F32).reshape(bsz, seq, N_QK_HEADS, HEAD_DIM)), rep, axis=2)
    k = jnp.repeat(l2_normalize(k.astype(F32).reshape(bsz, seq, N_QK_HEADS, HEAD_DIM)), rep, axis=2)
    v = v.astype(F32).reshape(bsz, seq, N_V_HEADS, HEAD_DIM)
    a = a.astype(F32).reshape(bsz, seq, N_DIR, N_V_HEADS)
    g = -jnp.exp(a_log.astype(F32)) * jax.nn.softplus(a + dt_bias.astype(F32))
    beta = jax.nn.sigmoid(b.astype(F32).reshape(bsz, seq, N_DIR, N_V_HEADS))
    o_fwd = chunk_gated_delta_rule(q, k, v, g[:, :, 0], beta[:, :, 0])
    flip = lambda t: jnp.flip(t, axis=1)
    o_bwd = flip(chunk_gated_delta_rule(flip(q), flip(k), flip(v), flip(g[:, :, 1]), flip(beta[:, :, 1])))
    o = rms_norm(o_fwd + o_bwd, norm_w) * jax.nn.silu(z.astype(F32).reshape(bsz, seq, N_V_HEADS, HEAD_DIM))
    return o.reshape(bsz, seq, V_WIDTH).astype(z.dtype) @ w_out


def conformer_conv(glu_in, conv_w, conv_b, ln_w, ln_b, w_out):
    val, gate = jnp.split(glu_in, 2, axis=-1)
    hcv = val * jax.nn.sigmoid(gate)
    hcv = depthwise_conv(hcv, conv_w) + conv_b.astype(hcv.dtype)
    hcv = jax.nn.silu(layer_norm(hcv, ln_w, ln_b))
    return hcv @ w_out


def encoder_layer(x, mix_norm_pre, w_in, short_conv_w, a_log, dt_bias, delta_norm_w, w_delta_out,
                  conf_conv_w, conf_conv_b, conf_ln_w, conf_ln_b, w_conf_out, w_mix_out,
                  mix_norm_post, ffn_norm_pre, w_up, ffn_conv_w, w_down, ffn_norm_post):
    h = rms_norm(x, mix_norm_pre)
    q, k, v, z, a, b, glu_in, gates = split_in_proj(h @ w_in)
    y_a = gated_deltanet_bidir(q, k, v, z, a, b, short_conv_w, a_log, dt_bias, delta_norm_w, w_delta_out)
    y_b = conformer_conv(glu_in, conf_conv_w, conf_conv_b, conf_ln_w, conf_ln_b, w_conf_out)
    gate_a, gate_b = jnp.split(jax.nn.sigmoid(gates), 2, axis=-1)
    merged = gate_a * y_a + gate_b * y_b
    x = x + rms_norm(merged @ w_mix_out, mix_norm_post)
    h = rms_norm(x, ffn_norm_pre)
    up = depthwise_conv(h @ w_up, ffn_conv_w)
    gt, val = jnp.split(up, 2, axis=-1)
    f = (jax.nn.silu(gt) * val) @ w_down
    return x + rms_norm(f, ffn_norm_post)


def run_trunk(x, params):
    for i in range(DEPTH):
        x = encoder_layer(x, *(p[i] for p in params))
    return x


def setup_inputs(seed: int = 0) -> dict:
    key = jax.random.key(seed)
    ks = jax.random.split(key, 24)

    def normal(k, shape, scale):
        return scale * jax.random.normal(k, shape, F32)

    def gain(k, shape):
        return 1.0 + 0.02 * jax.random.normal(k, shape, F32)

    dt = jnp.exp(jax.random.uniform(ks[5], (DEPTH, N_DIR, N_V_HEADS), F32, math.log(1e-3), math.log(1e-1)))
    return {
        'x_prompt': jax.random.normal(ks[0], (BATCH, SEQ, D_MODEL), F32),
        'x_sample': jax.random.normal(ks[1], (DEC_BATCH, DEC_SEQ, D_MODEL), F32),
        'mix_norm_pre': gain(ks[2], (DEPTH, D_MODEL)),
        'w_in': normal(ks[3], (DEPTH, D_MODEL, IN_COLS), D_MODEL ** -0.5),
        'short_conv_w': normal(ks[4], (DEPTH, SHORT_CONV, 2 * QK_WIDTH + V_WIDTH), SHORT_CONV ** -0.5),
        'a_log': jnp.log(jax.random.uniform(ks[6], (DEPTH, N_DIR, N_V_HEADS), F32, 1.0, 16.0)),
        'dt_bias': dt + jnp.log(-jnp.expm1(-dt)),
        'delta_norm_w': gain(ks[7], (DEPTH, HEAD_DIM)),
        'w_delta_out': normal(ks[8], (DEPTH, V_WIDTH, D_MODEL), V_WIDTH ** -0.5),
        'conf_conv_w': normal(ks[9], (DEPTH, CONF_KERNEL, CONF_WIDTH), CONF_KERNEL ** -0.5),
        'conf_conv_b': normal(ks[10], (DEPTH, CONF_WIDTH), 0.02),
        'conf_ln_w': gain(ks[11], (DEPTH, CONF_WIDTH)),
        'conf_ln_b': normal(ks[12], (DEPTH, CONF_WIDTH), 0.02),
        'w_conf_out': normal(ks[13], (DEPTH, CONF_WIDTH, D_MODEL), CONF_WIDTH ** -0.5),
        'w_mix_out': normal(ks[14], (DEPTH, D_MODEL, D_MODEL), D_MODEL ** -0.5),
        'mix_norm_post': gain(ks[15], (DEPTH, D_MODEL)),
        'ffn_norm_pre': gain(ks[16], (DEPTH, D_MODEL)),
        'w_up': normal(ks[17], (DEPTH, D_MODEL, 2 * D_FF), D_MODEL ** -0.5),
        'ffn_conv_w': normal(ks[18], (DEPTH, FFN_CONV, 2 * D_FF), FFN_CONV ** -0.5),
        'w_down': normal(ks[19], (DEPTH, D_FF, D_MODEL), D_FF ** -0.5),
        'ffn_norm_post': gain(ks[20], (DEPTH, D_MODEL)),
    }


def reference(x_prompt, x_sample, mix_norm_pre, w_in, short_conv_w, a_log, dt_bias, delta_norm_w,
              w_delta_out, conf_conv_w, conf_conv_b, conf_ln_w, conf_ln_b, w_conf_out, w_mix_out,
              mix_norm_post, ffn_norm_pre, w_up, ffn_conv_w, w_down, ffn_norm_post):
    params = (mix_norm_pre, w_in, short_conv_w, a_log, dt_bias, delta_norm_w, w_delta_out,
              conf_conv_w, conf_conv_b, conf_ln_w, conf_ln_b, w_conf_out, w_mix_out,
              mix_norm_post, ffn_norm_pre, w_up, ffn_conv_w, w_down, ffn_norm_post)
    y_prompt = run_trunk(x_prompt, params)
    y_sample = run_trunk(x_sample, params)
    return (y_prompt, y_sample)
```

```python
import functools

import jax
import jax.numpy as jnp
import numpy as np
from jax import lax
from jax.experimental import pallas as pl
from jax.experimental.pallas import tpu as pltpu

F32 = jnp.float32
BF16 = jnp.bfloat16

D_MODEL = 2048
HEAD_DIM = 128
N_QK_HEADS = 16
N_V_HEADS = 32
QK_WIDTH = N_QK_HEADS * HEAD_DIM
V_WIDTH = N_V_HEADS * HEAD_DIM
N_DIR = 2
SHORT_CONV = 5
CONF_KERNEL = 31
D_FF = 5632
FFN_CONV = 3
RMS_EPS = 1e-6
LN_EPS = 1e-5
L2_EPS = 1e-6

CHUNK = 128
HALO = 16
VMEM_LIMIT = 56 * 1024 * 1024
NEG_BIG = -1e30


def _cparams(sem):
    return pltpu.CompilerParams(dimension_semantics=sem, vmem_limit_bytes=VMEM_LIMIT)


def _sigmoid(x):
    return 1.0 / (1.0 + jnp.exp(-x))


def _silu(x):
    return x * _sigmoid(x)


def _dot(a, b):
    return jnp.dot(a, b, preferred_element_type=F32)


def _dot_nt(a, b):
    return lax.dot_general(a, b, (((1,), (1,)), ((), ())), preferred_element_type=F32)


def _dot_tn(a, b):
    return lax.dot_general(a, b, (((0,), (0,)), ((), ())), preferred_element_type=F32)


def _norm_mm_kernel(x_ref, g_ref, w_ref, o_ref, h_ref):
    @pl.when(pl.program_id(1) == 0)
    def _():
        x = x_ref[...]
        ms = jnp.mean(x * x, axis=-1, keepdims=True)
        h_ref[...] = (x * lax.rsqrt(ms + RMS_EPS) * g_ref[...]).astype(BF16)

    o_ref[...] = _dot(h_ref[...], w_ref[...]).astype(o_ref.dtype)


def _norm_matmul(x, gain, w, out_dtype, tm, tn):
    m, k = x.shape
    n = w.shape[1]
    return pl.pallas_call(
        _norm_mm_kernel,
        grid=(m // tm, n // tn),
        in_specs=[
            pl.BlockSpec((tm, k), lambda i, j: (i, 0)),
            pl.BlockSpec((1, k), lambda i, j: (0, 0)),
            pl.BlockSpec((k, tn), lambda i, j: (0, j)),
        ],
        out_specs=pl.BlockSpec((tm, tn), lambda i, j: (i, j)),
        out_shape=jax.ShapeDtypeStruct((m, n), out_dtype),
        scratch_shapes=[pltpu.VMEM((tm, k), BF16)],
        compiler_params=_cparams(("parallel", "arbitrary")),
    )(x, gain.reshape(1, k), w)


def _split3(x):
    hi = x.astype(BF16)
    r1 = x - hi.astype(F32)
    mid = r1.astype(BF16)
    lo = (r1 - mid.astype(F32)).astype(BF16)
    return hi, mid, lo


def _gate_table_kernel(x_ref, g_ref, w_ref, wt_ref, alog_ref, dtb_ref, alogt_ref, dtbt_ref,
                       col_ref, row_ref):
    tm = x_ref.shape[0]
    x = x_ref[...]
    ms = jnp.mean(x * x, axis=-1, keepdims=True)
    h = (x * lax.rsqrt(ms + RMS_EPS) * g_ref[...]).astype(BF16)

    def transform(p, a_log, dtb, is_beta):
        z = p + dtb
        softplus = jnp.maximum(z, 0.0) + jnp.log1p(jnp.exp(-jnp.abs(z)))
        return jnp.where(is_beta, _sigmoid(p), -jnp.exp(a_log) * softplus)

    ri = lax.broadcasted_iota(jnp.int32, (tm, tm), 0)
    ci = lax.broadcasted_iota(jnp.int32, (tm, tm), 1)
    same = (ri // CHUNK) == (ci // CHUNK)
    lower = jnp.where(same & (ri >= ci), 1.0, 0.0).astype(BF16)
    upper = jnp.where(same & (ri <= ci), 1.0, 0.0).astype(BF16)

    n_col = lax.broadcasted_iota(jnp.int32, (1, 128), 1)
    colv = transform(_dot(h, w_ref[...]), alog_ref[...], dtb_ref[...], (n_col % 4) >= 2)
    parts = _split3(colv)
    fwd = sum(_dot(lower, p) for p in parts)
    rev = sum(_dot(upper, p) for p in parts)
    cum = jnp.where(n_col < 64, fwd, rev)
    col_ref[...] = jnp.where((n_col % 4) >= 2, colv, cum)

    n_row = lax.broadcasted_iota(jnp.int32, (128, 1), 0)
    rowv = transform(_dot_nt(wt_ref[...], h), alogt_ref[...], dtbt_ref[...], (n_row % 4) >= 2)
    parts = _split3(rowv)
    fwd = sum(_dot(p, upper) for p in parts)
    rev = sum(_dot(p, lower) for p in parts)
    cum = jnp.where(n_row < 64, fwd, rev)
    rowv = jnp.where((n_row % 4) >= 2, rowv, cum)
    for p in range(tm // CHUNK):
        row_ref[p] = rowv[:, p * CHUNK:(p + 1) * CHUNK]


def _gate_tables(x, gain, w_ab, a_log_p, dtb_p, tm):
    m, k = x.shape
    return pl.pallas_call(
        _gate_table_kernel,
        grid=(m // tm,),
        in_specs=[
            pl.BlockSpec((tm, k), lambda i: (i, 0)),
            pl.BlockSpec((1, k), lambda i: (0, 0)),
            pl.BlockSpec((k, 128), lambda i: (0, 0)),
            pl.BlockSpec((128, k), lambda i: (0, 0)),
            pl.BlockSpec((1, 128), lambda i: (0, 0)),
            pl.BlockSpec((1, 128), lambda i: (0, 0)),
            pl.BlockSpec((128, 1), lambda i: (0, 0)),
            pl.BlockSpec((128, 1), lambda i: (0, 0)),
        ],
        out_specs=[
            pl.BlockSpec((tm, 128), lambda i: (i, 0)),
            pl.BlockSpec((tm // CHUNK, 128, CHUNK), lambda i: (i, 0, 0)),
        ],
        out_shape=[
            jax.ShapeDtypeStruct((m, 128), F32),
            jax.ShapeDtypeStruct((m // CHUNK, 128, CHUNK), F32),
        ],
        compiler_params=_cparams(("parallel",)),
    )(x, gain.reshape(1, k), w_ab, w_ab.T, a_log_p.reshape(1, 128), dtb_p.reshape(1, 128),
      a_log_p.reshape(128, 1), dtb_p.reshape(128, 1))


def _halo_specs(tt, tc, col_of, rows_total):
    r = tt // HALO
    last = rows_total // HALO - 1
    return [
        pl.BlockSpec((tt, tc), lambda i, j: (i, col_of(j))),
        pl.BlockSpec((HALO, tc), lambda i, j: (jnp.maximum(i * r - 1, 0), col_of(j))),
        pl.BlockSpec((HALO, tc), lambda i, j: (jnp.minimum((i + 1) * r, last), col_of(j))),
    ]


def _seq_edges(tiles_per_seq):
    t = pl.program_id(0) % tiles_per_seq
    return t == 0, t == tiles_per_seq - 1


def _fill_ext(ext_ref, main, prev, nxt, first, last):
    tt = main.shape[0]
    ext_ref[0:HALO, :] = jnp.where(first, 0.0, prev)
    ext_ref[HALO:HALO + tt, :] = main
    ext_ref[HALO + tt:HALO + tt + HALO, :] = jnp.where(last, 0.0, nxt)


def _conv_rows(ext_ref, w_ref, r0, rows, ksize):
    pad = (ksize - 1) // 2
    win = ext_ref[pl.ds(r0, rows + 2 * HALO), :]
    acc = None
    for d in range(ksize):
        off = HALO - pad + d
        term = w_ref[d:d + 1, :] * win[off:off + rows]
        acc = term if acc is None else acc + term
    return acc


def _short_conv_kernel(m_ref, p_ref, n_ref, w_ref, o_ref, ext_ref, *, tiles_per_seq, mode, sub):
    first, last = _seq_edges(tiles_per_seq)
    _fill_ext(ext_ref, m_ref[...].astype(F32), p_ref[...].astype(F32), n_ref[...].astype(F32),
              first, last)
    tt, tc = m_ref.shape

    def body(s, carry):
        r0 = pl.multiple_of(s * sub, sub)
        y = _silu(_conv_rows(ext_ref, w_ref, r0, sub, SHORT_CONV))
        if mode != "v":
            scale = HEAD_DIM ** -0.5 if mode == "q" else 1.0
            for hd in range(tc // HEAD_DIM):
                yh = y[:, hd * HEAD_DIM:(hd + 1) * HEAD_DIM]
                inv = lax.rsqrt(jnp.sum(yh * yh, axis=-1, keepdims=True) + L2_EPS)
                o_ref[pl.ds(r0, sub), hd * HEAD_DIM:(hd + 1) * HEAD_DIM] = (
                    (yh * inv) * scale).astype(o_ref.dtype)
        else:
            o_ref[pl.ds(r0, sub), :] = y.astype(o_ref.dtype)
        return carry

    lax.fori_loop(0, tt // sub, body, 0)


def _short_conv(proj, conv_w, col0, width, mode, seq, tt=512, tc=512, sub=16):
    rows = proj.shape[0]
    cb0 = col0 // tc
    return pl.pallas_call(
        functools.partial(_short_conv_kernel, tiles_per_seq=seq // tt, mode=mode, sub=sub),
        grid=(rows // tt, width // tc),
        in_specs=_halo_specs(tt, tc, lambda j: cb0 + j, rows) + [
            pl.BlockSpec((SHORT_CONV, tc), lambda i, j: (0, cb0 + j))],
        out_specs=pl.BlockSpec((tt, tc), lambda i, j: (i, j)),
        out_shape=jax.ShapeDtypeStruct((rows, width), BF16),
        scratch_shapes=[pltpu.VMEM((tt + 2 * HALO, tc), F32)],
        compiler_params=_cparams(("parallel", "parallel")),
    )(proj, proj, proj, conv_w)


def _conformer_kernel(vm_ref, vp_ref, vn_ref, gm_ref, gp_ref, gn_ref, w_ref, b_ref, lnw_ref,
                      lnb_ref, o_ref, ext_ref, *, tiles_per_seq, sub):
    first, last = _seq_edges(tiles_per_seq)

    def glu(v_ref, g_ref):
        return v_ref[...].astype(F32) * _sigmoid(g_ref[...].astype(F32))

    _fill_ext(ext_ref, glu(vm_ref, gm_ref), glu(vp_ref, gp_ref), glu(vn_ref, gn_ref), first, last)
    tt = vm_ref.shape[0]

    def body(s, carry):
        r0 = pl.multiple_of(s * sub, sub)
        y = _conv_rows(ext_ref, w_ref, r0, sub, CONF_KERNEL) + b_ref[...]
        mu = jnp.mean(y, axis=-1, keepdims=True)
        yc = y - mu
        var = jnp.mean(yc * yc, axis=-1, keepdims=True)
        z = yc * lax.rsqrt(var + LN_EPS) * lnw_ref[...] + lnb_ref[...]
        o_ref[pl.ds(r0, sub), :] = _silu(z).astype(o_ref.dtype)
        return carry

    lax.fori_loop(0, tt // sub, body, 0)


def _conformer_conv(proj, col_val, col_gate, conv_w, conv_b, ln_w, ln_b, seq, tt=256, sub=16):
    rows = proj.shape[0]
    tc = D_MODEL
    vec = pl.BlockSpec((1, tc), lambda i, j: (0, 0))
    return pl.pallas_call(
        functools.partial(_conformer_kernel, tiles_per_seq=seq // tt, sub=sub),
        grid=(rows // tt, 1),
        in_specs=(_halo_specs(tt, tc, lambda j: col_val // tc, rows)
                  + _halo_specs(tt, tc, lambda j: col_gate // tc, rows)
                  + [pl.BlockSpec((CONF_KERNEL, tc), lambda i, j: (0, 0)), vec, vec, vec]),
        out_specs=pl.BlockSpec((tt, tc), lambda i, j: (i, 0)),
        out_shape=jax.ShapeDtypeStruct((rows, tc), BF16),
        scratch_shapes=[pltpu.VMEM((tt + 2 * HALO, tc), F32)],
        compiler_params=_cparams(("parallel", "arbitrary")),
    )(proj, proj, proj, proj, proj, proj, conv_w, conv_b.reshape(1, tc), ln_w.reshape(1, tc),
      ln_b.reshape(1, tc))


def _block_mask(n):
    ri = lax.broadcasted_iota(jnp.int32, (CHUNK, CHUNK), 0)
    ci = lax.broadcasted_iota(jnp.int32, (CHUNK, CHUNK), 1)
    return (ri // n) == (ci // n)


def _unit_tri_inverse(l):
    ri = lax.broadcasted_iota(jnp.int32, (CHUNK, CHUNK), 0)
    ci = lax.broadcasted_iota(jnp.int32, (CHUNK, CHUNK), 1)
    eye = jnp.where(ri == ci, 1.0, 0.0).astype(F32)
    base = 16
    l0 = jnp.where(_block_mask(base), l, 0.0)
    x = eye - l0
    p = l0.astype(BF16)
    n = 2
    while n < base:
        p32 = _dot(p, p)
        p = p32.astype(BF16)
        x = x + _dot(x.astype(BF16), p)
        n *= 2
    n = base
    while n < CHUNK:
        c = jnp.where(_block_mask(2 * n) & jnp.logical_not(_block_mask(n)), l, 0.0).astype(BF16)
        xb = x.astype(BF16)
        x = x - _dot(xb, _dot(c, xb).astype(BF16))
        n *= 2
    return x


def _delta_kernel(q_ref, k_ref, v_ref, gcol_ref, grow_ref, o_ref, s_ref, *, reverse):
    tb = q_ref.shape[0]
    n_chunks = tb // CHUNK

    @pl.when(pl.program_id(2) == 0)
    def _():
        s_ref[...] = jnp.zeros_like(s_ref)

    ri = lax.broadcasted_iota(jnp.int32, (CHUNK, CHUNK), 0)
    ci = lax.broadcasted_iota(jnp.int32, (CHUNK, CHUNK), 1)
    incl = (ri <= ci) if reverse else (ri >= ci)
    strict = (ri < ci) if reverse else (ri > ci)
    edge = 0 if reverse else CHUNK - 1

    def chunk_step(step, carry):
        c = (n_chunks - 1 - step) if reverse else step
        r0 = pl.multiple_of(c * CHUNK, CHUNK)
        rows = pl.ds(r0, CHUNK)
        cols = gcol_ref[rows, :]
        rws = grow_ref[c]
        for a in range(2):
            q = q_ref[rows, a * HEAD_DIM:(a + 1) * HEAD_DIM]
            k = k_ref[rows, a * HEAD_DIM:(a + 1) * HEAD_DIM]
            kk = _dot_nt(k, k)
            qk = _dot_nt(q, k)
            kf = k.astype(F32)
            qf = q.astype(F32)
            for r in range(2):
                j = a * 4 + r
                idx = a * 2 + r
                gcol = cols[:, j:j + 1]
                bcol = cols[:, j + 2:j + 3]
                grow = rws[j:j + 1, :]
                decay = jnp.exp(jnp.where(incl, gcol - grow, NEG_BIG))
                lmat = jnp.where(strict, kk * decay * bcol, 0.0)
                tinv = _unit_tri_inverse(lmat)
                v = v_ref[rows, idx * HEAD_DIM:(idx + 1) * HEAD_DIM].astype(F32)
                eg = jnp.exp(gcol)
                rhs = jnp.concatenate([v * bcol, kf * (bcol * eg)], axis=1).astype(BF16)
                uw = _dot(tinv.astype(BF16), rhs)
                u = uw[:, :HEAD_DIM]
                w = uw[:, HEAD_DIM:]
                attn = (qk * decay).astype(BF16)
                gl = gcol[edge:edge + 1, :]
                k_dec = (kf * jnp.exp(gl - gcol)).astype(BF16)
                q_dec = (qf * eg).astype(BF16)
                state = s_ref[idx]
                ws = _dot(jnp.concatenate([w.astype(BF16), q_dec], axis=0), state.astype(BF16))
                v_new = u - ws[:CHUNK]
                v_new_b = v_new.astype(BF16)
                out = ws[CHUNK:] + _dot(attn, v_new_b)
                s_ref[idx] = state * jnp.exp(gl) + _dot_tn(k_dec, v_new_b)
                o_ref[rows, idx * HEAD_DIM:(idx + 1) * HEAD_DIM] = out.astype(o_ref.dtype)
        return carry

    lax.fori_loop(0, n_chunks, chunk_step, 0)


def _delta_scan(q, k, v, gcol3, grow3, direction, batch, seq, tb=512):
    rows = q.shape[0]
    nblk = seq // tb
    reverse = direction == 1

    def blk(b, i):
        return b * nblk + ((nblk - 1 - i) if reverse else i)

    return pl.pallas_call(
        functools.partial(_delta_kernel, reverse=reverse),
        grid=(batch, N_QK_HEADS // 2, nblk),
        in_specs=[
            pl.BlockSpec((tb, 2 * HEAD_DIM), lambda b, h, i: (blk(b, i), h)),
            pl.BlockSpec((tb, 2 * HEAD_DIM), lambda b, h, i: (blk(b, i), h)),
            pl.BlockSpec((tb, 4 * HEAD_DIM), lambda b, h, i: (blk(b, i), h)),
            pl.BlockSpec((None, tb, 8), lambda b, h, i: (direction * 8 + h, blk(b, i), 0)),
            pl.BlockSpec((tb // CHUNK, 8, CHUNK), lambda b, h, i: (blk(b, i), direction * 8 + h, 0)),
        ],
        out_specs=pl.BlockSpec((tb, 4 * HEAD_DIM), lambda b, h, i: (blk(b, i), h)),
        out_shape=jax.ShapeDtypeStruct((rows, V_WIDTH), F32),
        scratch_shapes=[pltpu.VMEM((4, HEAD_DIM, HEAD_DIM), F32)],
        compiler_params=_cparams(("parallel", "parallel", "arbitrary")),
    )(q, k, v, gcol3, grow3)


def _delta_out_kernel(of_ref, ob_ref, z_ref, nw_ref, w_ref, o_ref, a_ref):
    kstep = pl.program_id(1)
    tk = of_ref.shape[1]
    o = of_ref[...] + ob_ref[...]
    for hd in range(tk // HEAD_DIM):
        sl = slice(hd * HEAD_DIM, (hd + 1) * HEAD_DIM)
        oh = o[:, sl]
        ms = jnp.mean(oh * oh, axis=-1, keepdims=True)
        a_ref[:, sl] = ((oh * lax.rsqrt(ms + RMS_EPS) * nw_ref[...])
                        * _silu(z_ref[:, sl].astype(F32))).astype(BF16)
    part = _dot(a_ref[...], w_ref[...])

    @pl.when(kstep == 0)
    def _():
        o_ref[...] = part

    @pl.when(kstep != 0)
    def _():
        o_ref[...] += part


def _delta_out(o_f, o_b, proj, col_z, norm_w, w, tm=512, tk=1024):
    m = o_f.shape[0]
    n = w.shape[1]
    zb0 = col_z // tk
    return pl.pallas_call(
        _delta_out_kernel,
        grid=(m // tm, V_WIDTH // tk),
        in_specs=[
            pl.BlockSpec((tm, tk), lambda i, kk: (i, kk)),
            pl.BlockSpec((tm, tk), lambda i, kk: (i, kk)),
            pl.BlockSpec((tm, tk), lambda i, kk: (i, zb0 + kk)),
            pl.BlockSpec((1, HEAD_DIM), lambda i, kk: (0, 0)),
            pl.BlockSpec((tk, n), lambda i, kk: (kk, 0)),
        ],
        out_specs=pl.BlockSpec((tm, n), lambda i, kk: (i, 0)),
        out_shape=jax.ShapeDtypeStruct((m, n), F32),
        scratch_shapes=[pltpu.VMEM((tm, tk), BF16)],
        compiler_params=_cparams(("parallel", "arbitrary")),
    )(o_f, o_b, proj, norm_w.reshape(1, HEAD_DIM), w)


def _mm_kernel(a_ref, w_ref, o_ref):
    o_ref[...] = _dot(a_ref[...], w_ref[...]).astype(o_ref.dtype)


def _matmul(a, w, out_dtype, tm, tn):
    m, k = a.shape
    n = w.shape[1]
    return pl.pallas_call(
        _mm_kernel,
        grid=(m // tm, n // tn),
        in_specs=[pl.BlockSpec((tm, k), lambda i, j: (i, 0)),
                  pl.BlockSpec((k, tn), lambda i, j: (0, j))],
        out_specs=pl.BlockSpec((tm, tn), lambda i, j: (i, j)),
        out_shape=jax.ShapeDtypeStruct((m, n), out_dtype),
        compiler_params=_cparams(("parallel", "arbitrary")),
    )(a, w)


def _mix_kernel(ya_ref, yb_ref, ga_ref, gb_ref, x_ref, w_ref, g_ref, o_ref):
    merged = (_sigmoid(ga_ref[...].astype(F32)) * ya_ref[...]
              + _sigmoid(gb_ref[...].astype(F32)) * yb_ref[...]).astype(BF16)
    y = _dot(merged, w_ref[...])
    ms = jnp.mean(y * y, axis=-1, keepdims=True)
    o_ref[...] = x_ref[...] + y * lax.rsqrt(ms + RMS_EPS) * g_ref[...]


def _mix_out(y_a, y_b, proj, col_ga, col_gb, x, w, gain, tm=256):
    m, d = x.shape
    row = lambda i: (i, 0)
    return pl.pallas_call(
        _mix_kernel,
        grid=(m // tm,),
        in_specs=[
            pl.BlockSpec((tm, d), row),
            pl.BlockSpec((tm, d), row),
            pl.BlockSpec((tm, d), lambda i: (i, col_ga // d)),
            pl.BlockSpec((tm, d), lambda i: (i, col_gb // d)),
            pl.BlockSpec((tm, d), row),
            pl.BlockSpec((d, d), lambda i: (0, 0)),
            pl.BlockSpec((1, d), lambda i: (0, 0)),
        ],
        out_specs=pl.BlockSpec((tm, d), row),
        out_shape=jax.ShapeDtypeStruct((m, d), F32),
        compiler_params=_cparams(("parallel",)),
    )(y_a, y_b, proj, proj, x, w, gain.reshape(1, d))


def _ffn_down_kernel(gm_ref, gp_ref, gn_ref, vm_ref, vp_ref, vn_ref, cwg_ref, cwv_ref, w_ref,
                     x_ref, gain_ref, o_ref, extg_ref, extv_ref, act_ref, acc_ref,
                     *, tiles_per_seq, sub):
    kstep = pl.program_id(1)
    first, last = _seq_edges(tiles_per_seq)
    _fill_ext(extg_ref, gm_ref[...].astype(F32), gp_ref[...].astype(F32), gn_ref[...].astype(F32),
              first, last)
    _fill_ext(extv_ref, vm_ref[...].astype(F32), vp_ref[...].astype(F32), vn_ref[...].astype(F32),
              first, last)
    tm = gm_ref.shape[0]

    def body(s, carry):
        r0 = pl.multiple_of(s * sub, sub)
        gt = _conv_rows(extg_ref, cwg_ref, r0, sub, FFN_CONV)
        val = _conv_rows(extv_ref, cwv_ref, r0, sub, FFN_CONV)
        act_ref[pl.ds(r0, sub), :] = (_silu(gt) * val).astype(BF16)
        return carry

    lax.fori_loop(0, tm // sub, body, 0)
    part = _dot(act_ref[...], w_ref[...])

    @pl.when(kstep == 0)
    def _():
        acc_ref[...] = part

    @pl.when(kstep != 0)
    def _():
        acc_ref[...] += part

    @pl.when(kstep == pl.num_programs(1) - 1)
    def _():
        f = acc_ref[...]
        ms = jnp.mean(f * f, axis=-1, keepdims=True)
        o_ref[...] = x_ref[...] + f * lax.rsqrt(ms + RMS_EPS) * gain_ref[...]


def _ffn_down(up, conv_w, w, x, gain, seq, tm=512, tk=512, sub=16):
    m, d = x.shape
    nk = D_FF // tk
    return pl.pallas_call(
        functools.partial(_ffn_down_kernel, tiles_per_seq=seq // tm, sub=sub),
        grid=(m // tm, nk),
        in_specs=(_halo_specs(tm, tk, lambda kk: kk, m) + _halo_specs(tm, tk, lambda kk: nk + kk, m)
                  + [pl.BlockSpec((FFN_CONV, tk), lambda i, kk: (0, kk)),
                     pl.BlockSpec((FFN_CONV, tk), lambda i, kk: (0, nk + kk)),
                     pl.BlockSpec((tk, d), lambda i, kk: (kk, 0)),
                     pl.BlockSpec((tm, d), lambda i, kk: (i, 0)),
                     pl.BlockSpec((1, d), lambda i, kk: (0, 0))]),
        out_specs=pl.BlockSpec((tm, d), lambda i, kk: (i, 0)),
        out_shape=jax.ShapeDtypeStruct((m, d), F32),
        scratch_shapes=[pltpu.VMEM((tm + 2 * HALO, tk), F32), pltpu.VMEM((tm + 2 * HALO, tk), F32),
                        pltpu.VMEM((tm, tk), BF16), pltpu.VMEM((tm, d), F32)],
        compiler_params=_cparams(("parallel", "arbitrary")),
    )(up, up, up, up, up, up, conv_w, conv_w, w, x, gain.reshape(1, d))


_COL_Q = 0
_COL_K = QK_WIDTH
_COL_V = 2 * QK_WIDTH
_COL_Z = 2 * QK_WIDTH + V_WIDTH
_COL_GLU_V = 2 * QK_WIDTH + 2 * V_WIDTH
_COL_GLU_G = _COL_GLU_V + D_MODEL
_COL_GA = _COL_GLU_G + D_MODEL
_COL_GB = _COL_GA + D_MODEL
_AB_COL0 = 2 * QK_WIDTH + 2 * V_WIDTH
_AB_WIDTH = 2 * N_DIR * N_V_HEADS


def _table_permutation():
    src = np.zeros(128, np.int32)
    head = np.zeros(128, np.int32)
    for n in range(128):
        d, hp, a, kind, r = n // 64, (n % 64) // 8, (n % 8) // 4, (n % 4) // 2, n % 2
        vh = 4 * hp + 2 * a + r
        src[n] = kind * (N_DIR * N_V_HEADS) + d * N_V_HEADS + vh
        head[n] = d * N_V_HEADS + vh
    return src, head


def _encoder_layer(x3, p):
    batch, seq, d = x3.shape
    rows = batch * seq
    x = x3.reshape(rows, d)

    w_in = p["w_in"]
    w_main = jnp.concatenate([w_in[:, :_AB_COL0], w_in[:, _AB_COL0 + _AB_WIDTH:]], axis=1).astype(BF16)
    src, head = _table_permutation()
    w_ab = w_in[:, _AB_COL0:_AB_COL0 + _AB_WIDTH][:, src].astype(BF16)
    a_log_p = p["a_log"].reshape(-1)[head]
    dtb_p = p["dt_bias"].reshape(-1)[head]

    proj = _norm_matmul(x, p["mix_norm_pre"], w_main, BF16, tm=1024, tn=1024)
    gcol, grow3 = _gate_tables(x, p["mix_norm_pre"], w_ab, a_log_p, dtb_p, tm=512)
    gcol3 = gcol.reshape(rows, 16, 8).transpose(1, 0, 2)

    cw = p["short_conv_w"]
    q = _short_conv(proj, cw, _COL_Q, QK_WIDTH, "q", seq)
    k = _short_conv(proj, cw, _COL_K, QK_WIDTH, "k", seq)
    v = _short_conv(proj, cw, _COL_V, V_WIDTH, "v", seq)
    o_f = _delta_scan(q, k, v, gcol3, grow3, 0, batch, seq)
    o_b = _delta_scan(q, k, v, gcol3, grow3, 1, batch, seq)
    y_a = _delta_out(o_f, o_b, proj, _COL_Z, p["delta_norm_w"], p["w_delta_out"].astype(BF16))

    hcv = _conformer_conv(proj, _COL_GLU_V, _COL_GLU_G, p["conf_conv_w"], p["conf_conv_b"],
                          p["conf_ln_w"], p["conf_ln_b"], seq)
    y_b = _matmul(hcv, p["w_conf_out"].astype(BF16), F32, tm=1024, tn=1024)

    x1 = _mix_out(y_a, y_b, proj, _COL_GA, _COL_GB, x, p["w_mix_out"].astype(BF16),
                  p["mix_norm_post"])

    up = _norm_matmul(x1, p["ffn_norm_pre"], p["w_up"].astype(BF16), BF16, tm=1024, tn=1024)
    out = _ffn_down(up, p["ffn_conv_w"], p["w_down"].astype(BF16), x1, p["ffn_norm_post"], seq)
    return out.reshape(batch, seq, d)


_PARAM_NAMES = ("mix_norm_pre", "w_in", "short_conv_w", "a_log", "dt_bias", "delta_norm_w",
                "w_delta_out", "conf_conv_w", "conf_conv_b", "conf_ln_w", "conf_ln_b", "w_conf_out",
                "w_mix_out", "mix_norm_post", "ffn_norm_pre", "w_up", "ffn_conv_w", "w_down",
                "ffn_norm_post")


def kernel(x_prompt, x_sample, mix_norm_pre, w_in, short_conv_w, a_log, dt_bias, delta_norm_w,
           w_delta_out, conf_conv_w, conf_conv_b, conf_ln_w, conf_ln_b, w_conf_out, w_mix_out,
           mix_norm_post, ffn_norm_pre, w_up, ffn_conv_w, w_down, ffn_norm_post):
    stacked = (mix_norm_pre, w_in, short_conv_w, a_log, dt_bias, delta_norm_w, w_delta_out,
               conf_conv_w, conf_conv_b, conf_ln_w, conf_ln_b, w_conf_out, w_mix_out, mix_norm_post,
               ffn_norm_pre, w_up, ffn_conv_w, w_down, ffn_norm_post)
    depth = w_in.shape[0]
    outs = []
    for x in (x_prompt, x_sample):
        for layer in range(depth):
            x = _encoder_layer(x, {n: t[layer] for n, t in zip(_PARAM_NAMES, stacked)})
        outs.append(x)
    return tuple(outs)
```

```python
import functools

import jax
import jax.numpy as jnp
import numpy as np
from jax import lax
from jax.experimental import pallas as pl
from jax.experimental.pallas import tpu as pltpu

F32 = jnp.float32
BF16 = jnp.bfloat16

D_MODEL = 2048
HEAD_DIM = 128
N_QK_HEADS = 16
N_V_HEADS = 32
QK_WIDTH = N_QK_HEADS * HEAD_DIM
V_WIDTH = N_V_HEADS * HEAD_DIM
N_DIR = 2
SHORT_CONV = 5
CONF_KERNEL = 31
D_FF = 5632
FFN_CONV = 3
RMS_EPS = 1e-6
LN_EPS = 1e-5
L2_EPS = 1e-6

CHUNK = 128
HALO = 16
SUBLANES = 8
VMEM_LIMIT = 56 * 1024 * 1024
NEG_BIG = -1e30


def _cparams(sem):
    return pltpu.CompilerParams(dimension_semantics=sem, vmem_limit_bytes=VMEM_LIMIT)


def _sigmoid(x):
    return 1.0 / (1.0 + jnp.exp(-x))


def _silu(x):
    return x * _sigmoid(x)


def _dot(a, b):
    return jnp.dot(a, b, preferred_element_type=F32)


def _dot_nt(a, b):
    return lax.dot_general(a, b, (((1,), (1,)), ((), ())), preferred_element_type=F32)


def _dot_tn(a, b):
    return lax.dot_general(a, b, (((0,), (0,)), ((), ())), preferred_element_type=F32)


def _norm_mm_kernel(x_ref, g_ref, w_ref, o_ref, h_ref):
    @pl.when(pl.program_id(1) == 0)
    def _():
        x = x_ref[...]
        ms = jnp.mean(x * x, axis=-1, keepdims=True)
        h_ref[...] = (x * lax.rsqrt(ms + RMS_EPS) * g_ref[...]).astype(BF16)

    o_ref[...] = _dot(h_ref[...], w_ref[...]).astype(o_ref.dtype)


def _norm_matmul(x, gain, w, out_dtype, tm, tn):
    m, k = x.shape
    n = w.shape[1]
    return pl.pallas_call(
        _norm_mm_kernel,
        grid=(m // tm, n // tn),
        in_specs=[
            pl.BlockSpec((tm, k), lambda i, j: (i, 0)),
            pl.BlockSpec((1, k), lambda i, j: (0, 0)),
            pl.BlockSpec((k, tn), lambda i, j: (0, j)),
        ],
        out_specs=pl.BlockSpec((tm, tn), lambda i, j: (i, j)),
        out_shape=jax.ShapeDtypeStruct((m, n), out_dtype),
        scratch_shapes=[pltpu.VMEM((tm, k), BF16)],
        compiler_params=_cparams(("parallel", "arbitrary")),
    )(x, gain.reshape(1, k), w)


def _split3(x):
    hi = x.astype(BF16)
    r1 = x - hi.astype(F32)
    mid = r1.astype(BF16)
    lo = (r1 - mid.astype(F32)).astype(BF16)
    return hi, mid, lo


def _gate_table_kernel(x_ref, g_ref, w_ref, wt_ref, alog_ref, dtb_ref, alogt_ref, dtbt_ref,
                       col_ref, row_ref):
    tm = x_ref.shape[0]
    x = x_ref[...]
    ms = jnp.mean(x * x, axis=-1, keepdims=True)
    h = (x * lax.rsqrt(ms + RMS_EPS) * g_ref[...]).astype(BF16)

    def transform(p, a_log, dtb, is_beta):
        z = p + dtb
        softplus = jnp.maximum(z, 0.0) + jnp.log1p(jnp.exp(-jnp.abs(z)))
        return jnp.where(is_beta, _sigmoid(p), -jnp.exp(a_log) * softplus)

    ri = lax.broadcasted_iota(jnp.int32, (tm, tm), 0)
    ci = lax.broadcasted_iota(jnp.int32, (tm, tm), 1)
    same = (ri // CHUNK) == (ci // CHUNK)
    lower = jnp.where(same & (ri >= ci), 1.0, 0.0).astype(BF16)
    upper = jnp.where(same & (ri <= ci), 1.0, 0.0).astype(BF16)

    n_col = lax.broadcasted_iota(jnp.int32, (1, 128), 1)
    colv = transform(_dot(h, w_ref[...]), alog_ref[...], dtb_ref[...], (n_col % 4) >= 2)
    parts = _split3(colv)
    fwd = sum(_dot(lower, p) for p in parts)
    rev = sum(_dot(upper, p) for p in parts)
    cum = jnp.where(n_col < 64, fwd, rev)
    col_ref[...] = jnp.where((n_col % 4) >= 2, colv, cum)

    n_row = lax.broadcasted_iota(jnp.int32, (128, 1), 0)
    rowv = transform(_dot_nt(wt_ref[...], h), alogt_ref[...], dtbt_ref[...], (n_row % 4) >= 2)
    parts = _split3(rowv)
    fwd = sum(_dot(p, upper) for p in parts)
    rev = sum(_dot(p, lower) for p in parts)
    cum = jnp.where(n_row < 64, fwd, rev)
    rowv = jnp.where((n_row % 4) >= 2, rowv, cum)
    for p in range(tm // CHUNK):
        row_ref[p] = rowv[:, p * CHUNK:(p + 1) * CHUNK]


def _gate_tables(x, gain, w_ab, a_log_p, dtb_p, tm):
    m, k = x.shape
    return pl.pallas_call(
        _gate_table_kernel,
        grid=(m // tm,),
        in_specs=[
            pl.BlockSpec((tm, k), lambda i: (i, 0)),
            pl.BlockSpec((1, k), lambda i: (0, 0)),
            pl.BlockSpec((k, 128), lambda i: (0, 0)),
            pl.BlockSpec((128, k), lambda i: (0, 0)),
            pl.BlockSpec((1, 128), lambda i: (0, 0)),
            pl.BlockSpec((1, 128), lambda i: (0, 0)),
            pl.BlockSpec((128, 1), lambda i: (0, 0)),
            pl.BlockSpec((128, 1), lambda i: (0, 0)),
        ],
        out_specs=[
            pl.BlockSpec((tm, 128), lambda i: (i, 0)),
            pl.BlockSpec((tm // CHUNK, 128, CHUNK), lambda i: (i, 0, 0)),
        ],
        out_shape=[
            jax.ShapeDtypeStruct((m, 128), F32),
            jax.ShapeDtypeStruct((m // CHUNK, 128, CHUNK), F32),
        ],
        compiler_params=_cparams(("parallel",)),
    )(x, gain.reshape(1, k), w_ab, w_ab.T, a_log_p.reshape(1, 128), dtb_p.reshape(1, 128),
      a_log_p.reshape(128, 1), dtb_p.reshape(128, 1))


def _halo_specs(tt, tc, col_of, rows_total):
    r = tt // HALO
    last = rows_total // HALO - 1
    return [
        pl.BlockSpec((tt, tc), lambda i, j: (i, col_of(j))),
        pl.BlockSpec((HALO, tc), lambda i, j: (jnp.maximum(i * r - 1, 0), col_of(j))),
        pl.BlockSpec((HALO, tc), lambda i, j: (jnp.minimum((i + 1) * r, last), col_of(j))),
    ]


def _seq_edges(tiles_per_seq):
    t = pl.program_id(0) % tiles_per_seq
    return t == 0, t == tiles_per_seq - 1


def _fill_ext(ext_ref, main, prev, nxt, first, last):
    tt = main.shape[0]
    ext_ref[0:HALO, :] = jnp.where(first, 0.0, prev)
    ext_ref[HALO:HALO + tt, :] = main
    ext_ref[HALO + tt:HALO + tt + HALO, :] = jnp.where(last, 0.0, nxt)


def _conv_rows(ext_ref, w_ref, r0, rows, ksize):
    pad = (ksize - 1) // 2
    win = ext_ref[pl.ds(r0, rows + 2 * HALO), :]
    shifted = {}
    acc = None
    for d in range(ksize):
        off = HALO - pad + d
        res = off % SUBLANES
        if res not in shifted:
            span = rows + 2 * HALO - SUBLANES
            shifted[res] = win[res:res + span]
        base = off - res
        term = w_ref[d:d + 1, :] * shifted[res][base:base + rows]
        acc = term if acc is None else acc + term
    return acc


def _short_conv_kernel(m_ref, p_ref, n_ref, w_ref, o_ref, ext_ref, *, tiles_per_seq, mode, sub):
    first, last = _seq_edges(tiles_per_seq)
    _fill_ext(ext_ref, m_ref[...].astype(F32), p_ref[...].astype(F32), n_ref[...].astype(F32),
              first, last)
    tt, tc = m_ref.shape

    def body(s, carry):
        r0 = pl.multiple_of(s * sub, sub)
        y = _silu(_conv_rows(ext_ref, w_ref, r0, sub, SHORT_CONV))
        if mode != "v":
            scale = HEAD_DIM ** -0.5 if mode == "q" else 1.0
            for hd in range(tc // HEAD_DIM):
                yh = y[:, hd * HEAD_DIM:(hd + 1) * HEAD_DIM]
                inv = lax.rsqrt(jnp.sum(yh * yh, axis=-1, keepdims=True) + L2_EPS)
                o_ref[pl.ds(r0, sub), hd * HEAD_DIM:(hd + 1) * HEAD_DIM] = (
                    (yh * inv) * scale).astype(o_ref.dtype)
        else:
            o_ref[pl.ds(r0, sub), :] = y.astype(o_ref.dtype)
        return carry

    lax.fori_loop(0, tt // sub, body, 0, unroll=4)


def _short_conv(proj, conv_w, col0, width, mode, seq, tt=512, tc=512, sub=16):
    rows = proj.shape[0]
    cb0 = col0 // tc
    return pl.pallas_call(
        functools.partial(_short_conv_kernel, tiles_per_seq=seq // tt, mode=mode, sub=sub),
        grid=(rows // tt, width // tc),
        in_specs=_halo_specs(tt, tc, lambda j: cb0 + j, rows) + [
            pl.BlockSpec((SHORT_CONV, tc), lambda i, j: (0, cb0 + j))],
        out_specs=pl.BlockSpec((tt, tc), lambda i, j: (i, j)),
        out_shape=jax.ShapeDtypeStruct((rows, width), BF16),
        scratch_shapes=[pltpu.VMEM((tt + 2 * HALO, tc), F32)],
        compiler_params=_cparams(("parallel", "parallel")),
    )(proj, proj, proj, conv_w)


def _conformer_kernel(vm_ref, vp_ref, vn_ref, gm_ref, gp_ref, gn_ref, w_ref, b_ref, lnw_ref,
                      lnb_ref, o_ref, ext_ref, *, tiles_per_seq, sub):
    first, last = _seq_edges(tiles_per_seq)

    def glu(v_ref, g_ref):
        return v_ref[...].astype(F32) * _sigmoid(g_ref[...].astype(F32))

    _fill_ext(ext_ref, glu(vm_ref, gm_ref), glu(vp_ref, gp_ref), glu(vn_ref, gn_ref), first, last)
    tt = vm_ref.shape[0]

    def body(s, carry):
        r0 = pl.multiple_of(s * sub, sub)
        y = _conv_rows(ext_ref, w_ref, r0, sub, CONF_KERNEL) + b_ref[...]
        mu = jnp.mean(y, axis=-1, keepdims=True)
        yc = y - mu
        var = jnp.mean(yc * yc, axis=-1, keepdims=True)
        z = yc * lax.rsqrt(var + LN_EPS) * lnw_ref[...] + lnb_ref[...]
        o_ref[pl.ds(r0, sub), :] = _silu(z).astype(o_ref.dtype)
        return carry

    lax.fori_loop(0, tt // sub, body, 0)


def _conformer_conv(proj, col_val, col_gate, conv_w, conv_b, ln_w, ln_b, seq, tt=256, sub=16):
    rows = proj.shape[0]
    tc = D_MODEL
    vec = pl.BlockSpec((1, tc), lambda i, j: (0, 0))
    return pl.pallas_call(
        functools.partial(_conformer_kernel, tiles_per_seq=seq // tt, sub=sub),
        grid=(rows // tt, 1),
        in_specs=(_halo_specs(tt, tc, lambda j: col_val // tc, rows)
                  + _halo_specs(tt, tc, lambda j: col_gate // tc, rows)
                  + [pl.BlockSpec((CONF_KERNEL, tc), lambda i, j: (0, 0)), vec, vec, vec]),
        out_specs=pl.BlockSpec((tt, tc), lambda i, j: (i, 0)),
        out_shape=jax.ShapeDtypeStruct((rows, tc), BF16),
        scratch_shapes=[pltpu.VMEM((tt + 2 * HALO, tc), F32)],
        compiler_params=_cparams(("parallel", "arbitrary")),
    )(proj, proj, proj, proj, proj, proj, conv_w, conv_b.reshape(1, tc), ln_w.reshape(1, tc),
      ln_b.reshape(1, tc))


def _block_mask(n):
    ri = lax.broadcasted_iota(jnp.int32, (CHUNK, CHUNK), 0)
    ci = lax.broadcasted_iota(jnp.int32, (CHUNK, CHUNK), 1)
    return (ri // n) == (ci // n)


_NEUMANN_BLOCK = 16


def _delta_kernel(q_ref, k_ref, v_ref, gcol_ref, grow_ref, o_ref,
                  s_ref, l_ref, x_ref, p_ref, cx_ref, attn_ref, rhs_ref, kdt_ref, wq_ref, u_ref,
                  *, reverse):
    tb = q_ref.shape[0]
    n_chunks = tb // CHUNK
    items = [(c, a, r) for c in range(n_chunks) for a in range(2) for r in range(2)]

    def item(c, a, r):
        return c * 4 + a * 2 + r

    @pl.when(pl.program_id(2) == 0)
    def _():
        s_ref[...] = jnp.zeros_like(s_ref)

    ri = lax.broadcasted_iota(jnp.int32, (CHUNK, CHUNK), 0)
    ci = lax.broadcasted_iota(jnp.int32, (CHUNK, CHUNK), 1)
    incl = (ri <= ci) if reverse else (ri >= ci)
    strict = (ri < ci) if reverse else (ri > ci)
    eye = jnp.where(ri == ci, 1.0, 0.0).astype(F32)
    edge = 0 if reverse else CHUNK - 1

    for c in range(n_chunks):
        rows = pl.ds(c * CHUNK, CHUNK)
        cols = gcol_ref[rows, :]
        rws = grow_ref[c]
        for a in range(2):
            q = q_ref[rows, a * HEAD_DIM:(a + 1) * HEAD_DIM]
            k = k_ref[rows, a * HEAD_DIM:(a + 1) * HEAD_DIM]
            kk = _dot_nt(k, k)
            qk = _dot_nt(q, k)
            kf = k.astype(F32)
            qf = q.astype(F32)
            kt = kf.T
            for r in range(2):
                j = a * 4 + r
                it = item(c, a, r)
                gcol = cols[:, j:j + 1]
                bcol = cols[:, j + 2:j + 3]
                grow = rws[j:j + 1, :]
                decay = jnp.exp(jnp.where(incl, gcol - grow, NEG_BIG))
                lmat = jnp.where(strict, kk * decay * bcol, 0.0)
                l0 = jnp.where(_block_mask(_NEUMANN_BLOCK), lmat, 0.0)
                l_ref[it] = lmat
                x_ref[it] = eye - l0
                p_ref[it] = l0.astype(BF16)
                attn_ref[it] = (qk * decay).astype(BF16)
                v = v_ref[rows, (a * 2 + r) * HEAD_DIM:(a * 2 + r + 1) * HEAD_DIM].astype(F32)
                eg = jnp.exp(gcol)
                rhs_ref[it] = jnp.concatenate([v * bcol, kf * (bcol * eg)], axis=1).astype(BF16)
                gl = gcol[edge:edge + 1, :]
                kdt_ref[it] = (kt * jnp.exp(gl - grow)).astype(BF16)
                wq_ref[it, CHUNK:, :] = (qf * eg).astype(BF16)

    n_sq = _NEUMANN_BLOCK.bit_length() - 1
    for it in range(len(items)):
        p = p_ref[it]
        p_ref[it] = _dot(p, p).astype(BF16)
    for level in range(1, n_sq - 1):
        for it in range(len(items)):
            p = p_ref[it]
            px = _dot(jnp.concatenate([p, x_ref[it].astype(BF16)], axis=0), p)
            p_ref[it] = px[:CHUNK].astype(BF16)
            x_ref[it] = x_ref[it] + px[CHUNK:]
    for it in range(len(items)):
        x = x_ref[it]
        x_ref[it] = x + _dot(x.astype(BF16), p_ref[it])
    n = _NEUMANN_BLOCK
    while n < CHUNK:
        band = _block_mask(2 * n) & jnp.logical_not(_block_mask(n))
        for it in range(len(items)):
            c_band = jnp.where(band, l_ref[it], 0.0).astype(BF16)
            cx_ref[it] = _dot(c_band, x_ref[it].astype(BF16)).astype(BF16)
        for it in range(len(items)):
            x = x_ref[it]
            x_ref[it] = x - _dot(x.astype(BF16), cx_ref[it])
        n *= 2

    for it in range(len(items)):
        uw = _dot(x_ref[it].astype(BF16), rhs_ref[it])
        u_ref[it] = uw[:, :HEAD_DIM]
        wq_ref[it, :CHUNK, :] = uw[:, HEAD_DIM:].astype(BF16)

    for step in range(n_chunks):
        c = (n_chunks - 1 - step) if reverse else step
        rows = pl.ds(c * CHUNK, CHUNK)
        heads = [(a, r) for a in range(2) for r in range(2)]
        states = [s_ref[a * 2 + r] for a, r in heads]
        ws = [_dot(wq_ref[item(c, a, r)], s.astype(BF16)) for (a, r), s in zip(heads, states)]
        v_new = [(u_ref[item(c, a, r)] - w[:CHUNK]).astype(BF16) for (a, r), w in zip(heads, ws)]
        intra = [_dot(attn_ref[item(c, a, r)], vn) for (a, r), vn in zip(heads, v_new)]
        ds = [_dot(kdt_ref[item(c, a, r)], vn) for (a, r), vn in zip(heads, v_new)]
        edge_row = gcol_ref[c * CHUNK + edge:c * CHUNK + edge + 1, :]
        for n_h, (a, r) in enumerate(heads):
            idx = a * 2 + r
            j = a * 4 + r
            egl = jnp.exp(edge_row[:, j:j + 1])
            s_ref[idx] = states[n_h] * egl + ds[n_h]
            o_ref[rows, idx * HEAD_DIM:(idx + 1) * HEAD_DIM] = (
                ws[n_h][CHUNK:] + intra[n_h]).astype(o_ref.dtype)


def _delta_scan(q, k, v, gcol3, grow3, direction, batch, seq, tb=512):
    rows = q.shape[0]
    nblk = seq // tb
    reverse = direction == 1
    n_items = 4 * (tb // CHUNK)

    def blk(b, i):
        return b * nblk + ((nblk - 1 - i) if reverse else i)

    return pl.pallas_call(
        functools.partial(_delta_kernel, reverse=reverse),
        grid=(batch, N_QK_HEADS // 2, nblk),
        in_specs=[
            pl.BlockSpec((tb, 2 * HEAD_DIM), lambda b, h, i: (blk(b, i), h)),
            pl.BlockSpec((tb, 2 * HEAD_DIM), lambda b, h, i: (blk(b, i), h)),
            pl.BlockSpec((tb, 4 * HEAD_DIM), lambda b, h, i: (blk(b, i), h)),
            pl.BlockSpec((None, tb, 8), lambda b, h, i: (direction * 8 + h, blk(b, i), 0)),
            pl.BlockSpec((tb // CHUNK, 8, CHUNK), lambda b, h, i: (blk(b, i), direction * 8 + h, 0)),
        ],
        out_specs=pl.BlockSpec((tb, 4 * HEAD_DIM), lambda b, h, i: (blk(b, i), h)),
        out_shape=jax.ShapeDtypeStruct((rows, V_WIDTH), F32),
        scratch_shapes=[
            pltpu.VMEM((4, HEAD_DIM, HEAD_DIM), F32),
            pltpu.VMEM((n_items, CHUNK, CHUNK), F32),
            pltpu.VMEM((n_items, CHUNK, CHUNK), F32),
            pltpu.VMEM((n_items, CHUNK, CHUNK), BF16),
            pltpu.VMEM((n_items, CHUNK, CHUNK), BF16),
            pltpu.VMEM((n_items, CHUNK, CHUNK), BF16),
            pltpu.VMEM((n_items, CHUNK, 2 * HEAD_DIM), BF16),
            pltpu.VMEM((n_items, HEAD_DIM, CHUNK), BF16),
            pltpu.VMEM((n_items, 2 * CHUNK, HEAD_DIM), BF16),
            pltpu.VMEM((n_items, CHUNK, HEAD_DIM), F32),
        ],
        compiler_params=_cparams(("parallel", "parallel", "arbitrary")),
    )(q, k, v, gcol3, grow3)


def _delta_out_kernel(of_ref, ob_ref, z_ref, nw_ref, w_ref, o_ref, a_ref):
    kstep = pl.program_id(1)
    tk = of_ref.shape[1]
    o = of_ref[...] + ob_ref[...]
    for hd in range(tk // HEAD_DIM):
        sl = slice(hd * HEAD_DIM, (hd + 1) * HEAD_DIM)
        oh = o[:, sl]
        ms = jnp.mean(oh * oh, axis=-1, keepdims=True)
        a_ref[:, sl] = ((oh * lax.rsqrt(ms + RMS_EPS) * nw_ref[...])
                        * _silu(z_ref[:, sl].astype(F32))).astype(BF16)
    part = _dot(a_ref[...], w_ref[...])

    @pl.when(kstep == 0)
    def _():
        o_ref[...] = part

    @pl.when(kstep != 0)
    def _():
        o_ref[...] += part


def _delta_out(o_f, o_b, proj, col_z, norm_w, w, tm=512, tk=1024):
    m = o_f.shape[0]
    n = w.shape[1]
    zb0 = col_z // tk
    return pl.pallas_call(
        _delta_out_kernel,
        grid=(m // tm, V_WIDTH // tk),
        in_specs=[
            pl.BlockSpec((tm, tk), lambda i, kk: (i, kk)),
            pl.BlockSpec((tm, tk), lambda i, kk: (i, kk)),
            pl.BlockSpec((tm, tk), lambda i, kk: (i, zb0 + kk)),
            pl.BlockSpec((1, HEAD_DIM), lambda i, kk: (0, 0)),
            pl.BlockSpec((tk, n), lambda i, kk: (kk, 0)),
        ],
        out_specs=pl.BlockSpec((tm, n), lambda i, kk: (i, 0)),
        out_shape=jax.ShapeDtypeStruct((m, n), F32),
        scratch_shapes=[pltpu.VMEM((tm, tk), BF16)],
        compiler_params=_cparams(("parallel", "arbitrary")),
    )(o_f, o_b, proj, norm_w.reshape(1, HEAD_DIM), w)


def _mm_kernel(a_ref, w_ref, o_ref):
    o_ref[...] = _dot(a_ref[...], w_ref[...]).astype(o_ref.dtype)


def _matmul(a, w, out_dtype, tm, tn):
    m, k = a.shape
    n = w.shape[1]
    return pl.pallas_call(
        _mm_kernel,
        grid=(m // tm, n // tn),
        in_specs=[pl.BlockSpec((tm, k), lambda i, j: (i, 0)),
                  pl.BlockSpec((k, tn), lambda i, j: (0, j))],
        out_specs=pl.BlockSpec((tm, tn), lambda i, j: (i, j)),
        out_shape=jax.ShapeDtypeStruct((m, n), out_dtype),
        compiler_params=_cparams(("parallel", "arbitrary")),
    )(a, w)


def _mix_kernel(ya_ref, yb_ref, ga_ref, gb_ref, x_ref, w_ref, g_ref, o_ref):
    merged = (_sigmoid(ga_ref[...].astype(F32)) * ya_ref[...]
              + _sigmoid(gb_ref[...].astype(F32)) * yb_ref[...]).astype(BF16)
    y = _dot(merged, w_ref[...])
    ms = jnp.mean(y * y, axis=-1, keepdims=True)
    o_ref[...] = x_ref[...] + y * lax.rsqrt(ms + RMS_EPS) * g_ref[...]


def _mix_out(y_a, y_b, proj, col_ga, col_gb, x, w, gain, tm=256):
    m, d = x.shape
    row = lambda i: (i, 0)
    return pl.pallas_call(
        _mix_kernel,
        grid=(m // tm,),
        in_specs=[
            pl.BlockSpec((tm, d), row),
            pl.BlockSpec((tm, d), row),
            pl.BlockSpec((tm, d), lambda i: (i, col_ga // d)),
            pl.BlockSpec((tm, d), lambda i: (i, col_gb // d)),
            pl.BlockSpec((tm, d), row),
            pl.BlockSpec((d, d), lambda i: (0, 0)),
            pl.BlockSpec((1, d), lambda i: (0, 0)),
        ],
        out_specs=pl.BlockSpec((tm, d), row),
        out_shape=jax.ShapeDtypeStruct((m, d), F32),
        compiler_params=_cparams(("parallel",)),
    )(y_a, y_b, proj, proj, x, w, gain.reshape(1, d))


def _ffn_down_kernel(gm_ref, gp_ref, gn_ref, vm_ref, vp_ref, vn_ref, cwg_ref, cwv_ref, w_ref,
                     x_ref, gain_ref, o_ref, extg_ref, extv_ref, acc_ref,
                     *, tiles_per_seq, sub):
    kstep = pl.program_id(1)
    first, last = _seq_edges(tiles_per_seq)
    _fill_ext(extg_ref, gm_ref[...].astype(F32), gp_ref[...].astype(F32), gn_ref[...].astype(F32),
              first, last)
    _fill_ext(extv_ref, vm_ref[...].astype(F32), vp_ref[...].astype(F32), vn_ref[...].astype(F32),
              first, last)
    tm = gm_ref.shape[0]

    @pl.when(kstep == 0)
    def _():
        acc_ref[...] = jnp.zeros_like(acc_ref)

    for s in range(tm // sub):
        r0 = s * sub
        gt = _conv_rows(extg_ref, cwg_ref, r0, sub, FFN_CONV)
        val = _conv_rows(extv_ref, cwv_ref, r0, sub, FFN_CONV)
        act = (_silu(gt) * val).astype(BF16)
        acc_ref[r0:r0 + sub, :] += _dot(act, w_ref[...])

    @pl.when(kstep == pl.num_programs(1) - 1)
    def _():
        f = acc_ref[...]
        ms = jnp.mean(f * f, axis=-1, keepdims=True)
        o_ref[...] = x_ref[...] + f * lax.rsqrt(ms + RMS_EPS) * gain_ref[...]


def _ffn_down(up, conv_w, w, x, gain, seq, tm=512, tk=512, sub=128):
    m, d = x.shape
    nk = D_FF // tk
    return pl.pallas_call(
        functools.partial(_ffn_down_kernel, tiles_per_seq=seq // tm, sub=sub),
        grid=(m // tm, nk),
        in_specs=(_halo_specs(tm, tk, lambda kk: kk, m) + _halo_specs(tm, tk, lambda kk: nk + kk, m)
                  + [pl.BlockSpec((FFN_CONV, tk), lambda i, kk: (0, kk)),
                     pl.BlockSpec((FFN_CONV, tk), lambda i, kk: (0, nk + kk)),
                     pl.BlockSpec((tk, d), lambda i, kk: (kk, 0)),
                     pl.BlockSpec((tm, d), lambda i, kk: (i, 0)),
                     pl.BlockSpec((1, d), lambda i, kk: (0, 0))]),
        out_specs=pl.BlockSpec((tm, d), lambda i, kk: (i, 0)),
        out_shape=jax.ShapeDtypeStruct((m, d), F32),
        scratch_shapes=[pltpu.VMEM((tm + 2 * HALO, tk), F32), pltpu.VMEM((tm + 2 * HALO, tk), F32),
                        pltpu.VMEM((tm, d), F32)],
        compiler_params=_cparams(("parallel", "arbitrary")),
    )(up, up, up, up, up, up, conv_w, conv_w, w, x, gain.reshape(1, d))


_COL_Q = 0
_COL_K = QK_WIDTH
_COL_V = 2 * QK_WIDTH
_COL_Z = 2 * QK_WIDTH + V_WIDTH
_COL_GLU_V = 2 * QK_WIDTH + 2 * V_WIDTH
_COL_GLU_G = _COL_GLU_V + D_MODEL
_COL_GA = _COL_GLU_G + D_MODEL
_COL_GB = _COL_GA + D_MODEL
_AB_COL0 = 2 * QK_WIDTH + 2 * V_WIDTH
_AB_WIDTH = 2 * N_DIR * N_V_HEADS


def _table_permutation():
    src = np.zeros(128, np.int32)
    head = np.zeros(128, np.int32)
    for n in range(128):
        d, hp, a, kind, r = n // 64, (n % 64) // 8, (n % 8) // 4, (n % 4) // 2, n % 2
        vh = 4 * hp + 2 * a + r
        src[n] = kind * (N_DIR * N_V_HEADS) + d * N_V_HEADS + vh
        head[n] = d * N_V_HEADS + vh
    return src, head


def _encoder_layer(x3, p):
    batch, seq, d = x3.shape
    rows = batch * seq
    x = x3.reshape(rows, d)

    w_in = p["w_in"]
    w_main = jnp.concatenate([w_in[:, :_AB_COL0], w_in[:, _AB_COL0 + _AB_WIDTH:]], axis=1).astype(BF16)
    src, head = _table_permutation()
    w_ab = w_in[:, _AB_COL0:_AB_COL0 + _AB_WIDTH][:, src].astype(BF16)
    a_log_p = p["a_log"].reshape(-1)[head]
    dtb_p = p["dt_bias"].reshape(-1)[head]

    proj = _norm_matmul(x, p["mix_norm_pre"], w_main, BF16, tm=1024, tn=1024)
    gcol, grow3 = _gate_tables(x, p["mix_norm_pre"], w_ab, a_log_p, dtb_p, tm=512)
    gcol3 = gcol.reshape(rows, 16, 8).transpose(1, 0, 2)

    cw = p["short_conv_w"]
    q = _short_conv(proj, cw, _COL_Q, QK_WIDTH, "q", seq)
    k = _short_conv(proj, cw, _COL_K, QK_WIDTH, "k", seq)
    v = _short_conv(proj, cw, _COL_V, V_WIDTH, "v", seq)
    o_f = _delta_scan(q, k, v, gcol3, grow3, 0, batch, seq)
    o_b = _delta_scan(q, k, v, gcol3, grow3, 1, batch, seq)
    y_a = _delta_out(o_f, o_b, proj, _COL_Z, p["delta_norm_w"], p["w_delta_out"].astype(BF16))

    hcv = _conformer_conv(proj, _COL_GLU_V, _COL_GLU_G, p["conf_conv_w"], p["conf_conv_b"],
                          p["conf_ln_w"], p["conf_ln_b"], seq)
    y_b = _matmul(hcv, p["w_conf_out"].astype(BF16), F32, tm=1024, tn=1024)

    x1 = _mix_out(y_a, y_b, proj, _COL_GA, _COL_GB, x, p["w_mix_out"].astype(BF16),
                  p["mix_norm_post"])

    up = _norm_matmul(x1, p["ffn_norm_pre"], p["w_up"].astype(BF16), BF16, tm=1024, tn=1024)
    out = _ffn_down(up, p["ffn_conv_w"], p["w_down"].astype(BF16), x1, p["ffn_norm_post"], seq)
    return out.reshape(batch, seq, d)


_PARAM_NAMES = ("mix_norm_pre", "w_in", "short_conv_w", "a_log", "dt_bias", "delta_norm_w",
                "w_delta_out", "conf_conv_w", "conf_conv_b", "conf_ln_w", "conf_ln_b", "w_conf_out",
                "w_mix_out", "mix_norm_post", "ffn_norm_pre", "w_up", "ffn_conv_w", "w_down",
                "ffn_norm_post")


def kernel(x_prompt, x_sample, mix_norm_pre, w_in, short_conv_w, a_log, dt_bias, delta_norm_w,
           w_delta_out, conf_conv_w, conf_conv_b, conf_ln_w, conf_ln_b, w_conf_out, w_mix_out,
           mix_norm_post, ffn_norm_pre, w_up, ffn_conv_w, w_down, ffn_norm_post):
    stacked = (mix_norm_pre, w_in, short_conv_w, a_log, dt_bias, delta_norm_w, w_delta_out,
               conf_conv_w, conf_conv_b, conf_ln_w, conf_ln_b, w_conf_out, w_mix_out, mix_norm_post,
               ffn_norm_pre, w_up, ffn_conv_w, w_down, ffn_norm_post)
    depth = w_in.shape[0]
    outs = []
    for x in (x_prompt, x_sample):
        for layer in range(depth):
            x = _encoder_layer(x, {n: t[layer] for n, t in zip(_PARAM_NAMES, stacked)})
        outs.append(x)
    return tuple(outs)
```

```python
import functools

import jax
import jax.numpy as jnp
import numpy as np
from jax import lax
from jax.experimental import pallas as pl
from jax.experimental.pallas import tpu as pltpu

F32 = jnp.float32
BF16 = jnp.bfloat16

D_MODEL = 2048
HEAD_DIM = 128
N_QK_HEADS = 16
N_V_HEADS = 32
QK_WIDTH = N_QK_HEADS * HEAD_DIM
V_WIDTH = N_V_HEADS * HEAD_DIM
N_DIR = 2
SHORT_CONV = 5
CONF_KERNEL = 31
D_FF = 5632
FFN_CONV = 3
RMS_EPS = 1e-6
LN_EPS = 1e-5
L2_EPS = 1e-6

CHUNK = 128
HALO = 16
SUBLANES = 8
LANES = 128
VMEM_LIMIT = 56 * 1024 * 1024
NEG_BIG = -1e30


def _cparams(sem):
    return pltpu.CompilerParams(dimension_semantics=sem, vmem_limit_bytes=VMEM_LIMIT)


def _sigmoid(x):
    return 1.0 / (1.0 + jnp.exp(-x))


def _silu(x):
    return x * _sigmoid(x)


def _dot(a, b):
    return jnp.dot(a, b, preferred_element_type=F32)


def _dot_nt(a, b):
    return lax.dot_general(a, b, (((1,), (1,)), ((), ())), preferred_element_type=F32)


def _dot_tn(a, b):
    return lax.dot_general(a, b, (((0,), (0,)), ((), ())), preferred_element_type=F32)


def _norm_mm_kernel(x_ref, g_ref, w_ref, o_ref, h_ref):
    @pl.when(pl.program_id(1) == 0)
    def _():
        x = x_ref[...]
        ms = jnp.mean(x * x, axis=-1, keepdims=True)
        h_ref[...] = (x * lax.rsqrt(ms + RMS_EPS) * g_ref[...]).astype(BF16)

    o_ref[...] = _dot(h_ref[...], w_ref[...]).astype(o_ref.dtype)


def _norm_matmul(x, gain, w, out_dtype, tm, tn):
    m, k = x.shape
    n = w.shape[1]
    return pl.pallas_call(
        _norm_mm_kernel,
        grid=(m // tm, n // tn),
        in_specs=[
            pl.BlockSpec((tm, k), lambda i, j: (i, 0)),
            pl.BlockSpec((1, k), lambda i, j: (0, 0)),
            pl.BlockSpec((k, tn), lambda i, j: (0, j)),
        ],
        out_specs=pl.BlockSpec((tm, tn), lambda i, j: (i, j)),
        out_shape=jax.ShapeDtypeStruct((m, n), out_dtype),
        scratch_shapes=[pltpu.VMEM((tm, k), BF16)],
        compiler_params=_cparams(("parallel", "arbitrary")),
    )(x, gain.reshape(1, k), w)


def _split3(x):
    hi = x.astype(BF16)
    r1 = x - hi.astype(F32)
    mid = r1.astype(BF16)
    lo = (r1 - mid.astype(F32)).astype(BF16)
    return hi, mid, lo


def _gate_table_kernel(x_ref, g_ref, w_ref, wt_ref, alog_ref, dtb_ref, alogt_ref, dtbt_ref,
                       col_ref, row_ref):
    tm = x_ref.shape[0]
    x = x_ref[...]
    ms = jnp.mean(x * x, axis=-1, keepdims=True)
    h = (x * lax.rsqrt(ms + RMS_EPS) * g_ref[...]).astype(BF16)

    def transform(p, a_log, dtb, is_beta):
        z = p + dtb
        softplus = jnp.maximum(z, 0.0) + jnp.log1p(jnp.exp(-jnp.abs(z)))
        return jnp.where(is_beta, _sigmoid(p), -jnp.exp(a_log) * softplus)

    ri = lax.broadcasted_iota(jnp.int32, (tm, tm), 0)
    ci = lax.broadcasted_iota(jnp.int32, (tm, tm), 1)
    same = (ri // CHUNK) == (ci // CHUNK)
    lower = jnp.where(same & (ri >= ci), 1.0, 0.0).astype(BF16)
    upper = jnp.where(same & (ri <= ci), 1.0, 0.0).astype(BF16)

    n_col = lax.broadcasted_iota(jnp.int32, (1, 128), 1)
    colv = transform(_dot(h, w_ref[...]), alog_ref[...], dtb_ref[...], (n_col % 4) >= 2)
    parts = _split3(colv)
    fwd = sum(_dot(lower, p) for p in parts)
    rev = sum(_dot(upper, p) for p in parts)
    cum = jnp.where(n_col < 64, fwd, rev)
    col_ref[...] = jnp.where((n_col % 4) >= 2, colv, cum)

    n_row = lax.broadcasted_iota(jnp.int32, (128, 1), 0)
    rowv = transform(_dot_nt(wt_ref[...], h), alogt_ref[...], dtbt_ref[...], (n_row % 4) >= 2)
    parts = _split3(rowv)
    fwd = sum(_dot(p, upper) for p in parts)
    rev = sum(_dot(p, lower) for p in parts)
    cum = jnp.where(n_row < 64, fwd, rev)
    rowv = jnp.where((n_row % 4) >= 2, rowv, cum)
    for p in range(tm // CHUNK):
        row_ref[p] = rowv[:, p * CHUNK:(p + 1) * CHUNK]


def _gate_tables(x, gain, w_ab, a_log_p, dtb_p, tm):
    m, k = x.shape
    return pl.pallas_call(
        _gate_table_kernel,
        grid=(m // tm,),
        in_specs=[
            pl.BlockSpec((tm, k), lambda i: (i, 0)),
            pl.BlockSpec((1, k), lambda i: (0, 0)),
            pl.BlockSpec((k, 128), lambda i: (0, 0)),
            pl.BlockSpec((128, k), lambda i: (0, 0)),
            pl.BlockSpec((1, 128), lambda i: (0, 0)),
            pl.BlockSpec((1, 128), lambda i: (0, 0)),
            pl.BlockSpec((128, 1), lambda i: (0, 0)),
            pl.BlockSpec((128, 1), lambda i: (0, 0)),
        ],
        out_specs=[
            pl.BlockSpec((tm, 128), lambda i: (i, 0)),
            pl.BlockSpec((tm // CHUNK, 128, CHUNK), lambda i: (i, 0, 0)),
        ],
        out_shape=[
            jax.ShapeDtypeStruct((m, 128), F32),
            jax.ShapeDtypeStruct((m // CHUNK, 128, CHUNK), F32),
        ],
        compiler_params=_cparams(("parallel",)),
    )(x, gain.reshape(1, k), w_ab, w_ab.T, a_log_p.reshape(1, 128), dtb_p.reshape(1, 128),
      a_log_p.reshape(128, 1), dtb_p.reshape(128, 1))


def _halo_specs(tt, tc, col_of, rows_total):
    r = tt // HALO
    last = rows_total // HALO - 1
    return [
        pl.BlockSpec((tt, tc), lambda i, j: (i, col_of(j))),
        pl.BlockSpec((HALO, tc), lambda i, j: (jnp.maximum(i * r - 1, 0), col_of(j))),
        pl.BlockSpec((HALO, tc), lambda i, j: (jnp.minimum((i + 1) * r, last), col_of(j))),
    ]


def _seq_edges(tiles_per_seq):
    t = pl.program_id(0) % tiles_per_seq
    return t == 0, t == tiles_per_seq - 1


def _fill_ext(ext_ref, main, prev, nxt, first, last):
    tt = main.shape[0]
    ext_ref[0:HALO, :] = jnp.where(first, 0.0, prev)
    ext_ref[HALO:HALO + tt, :] = main
    ext_ref[HALO + tt:HALO + tt + HALO, :] = jnp.where(last, 0.0, nxt)


def _conv_rows(ext_ref, w_ref, r0, rows, ksize):
    cols = [_conv_col(ext_ref, w_ref, r0, rows, ksize, j) for j in range(ext_ref.shape[1] // LANES)]
    return cols[0] if len(cols) == 1 else jnp.concatenate(cols, axis=1)


def _conv_col(ext_ref, w_ref, r0, rows, ksize, j):
    pad = (ksize - 1) // 2
    lanes = slice(j * LANES, (j + 1) * LANES)
    win = ext_ref[pl.ds(r0, rows + 2 * HALO), lanes]
    shifted = {}
    acc = None
    for d in range(ksize):
        off = HALO - pad + d
        res = off % SUBLANES
        if res not in shifted:
            shifted[res] = win if res == 0 else pltpu.roll(win, rows + 2 * HALO - res, 0)
        base = off - res
        term = w_ref[d:d + 1, lanes] * shifted[res][base:base + rows]
        acc = term if acc is None else acc + term
    return acc


def _short_conv_kernel(m_ref, p_ref, n_ref, w_ref, o_ref, ext_ref, *, tiles_per_seq, mode, sub):
    first, last = _seq_edges(tiles_per_seq)
    _fill_ext(ext_ref, m_ref[...].astype(F32), p_ref[...].astype(F32), n_ref[...].astype(F32),
              first, last)
    tt, tc = m_ref.shape

    def body(s, carry):
        r0 = pl.multiple_of(s * sub, sub)
        y = _silu(_conv_rows(ext_ref, w_ref, r0, sub, SHORT_CONV))
        if mode != "v":
            scale = HEAD_DIM ** -0.5 if mode == "q" else 1.0
            for hd in range(tc // HEAD_DIM):
                yh = y[:, hd * HEAD_DIM:(hd + 1) * HEAD_DIM]
                inv = lax.rsqrt(jnp.sum(yh * yh, axis=-1, keepdims=True) + L2_EPS)
                o_ref[pl.ds(r0, sub), hd * HEAD_DIM:(hd + 1) * HEAD_DIM] = (
                    (yh * inv) * scale).astype(o_ref.dtype)
        else:
            o_ref[pl.ds(r0, sub), :] = y.astype(o_ref.dtype)
        return carry

    lax.fori_loop(0, tt // sub, body, 0, unroll=4)


def _short_conv(proj, conv_w, col0, width, mode, seq, tt=512, tc=512, sub=16):
    rows = proj.shape[0]
    cb0 = col0 // tc
    return pl.pallas_call(
        functools.partial(_short_conv_kernel, tiles_per_seq=seq // tt, mode=mode, sub=sub),
        grid=(rows // tt, width // tc),
        in_specs=_halo_specs(tt, tc, lambda j: cb0 + j, rows) + [
            pl.BlockSpec((SHORT_CONV, tc), lambda i, j: (0, cb0 + j))],
        out_specs=pl.BlockSpec((tt, tc), lambda i, j: (i, j)),
        out_shape=jax.ShapeDtypeStruct((rows, width), BF16),
        scratch_shapes=[pltpu.VMEM((tt + 2 * HALO, tc), F32)],
        compiler_params=_cparams(("parallel", "parallel")),
    )(proj, proj, proj, conv_w)


def _conformer_kernel(vm_ref, vp_ref, vn_ref, gm_ref, gp_ref, gn_ref, w_ref, b_ref, lnw_ref,
                      lnb_ref, o_ref, ext_ref, *, tiles_per_seq, sub):
    first, last = _seq_edges(tiles_per_seq)

    def glu(v_ref, g_ref):
        return v_ref[...].astype(F32) * _sigmoid(g_ref[...].astype(F32))

    _fill_ext(ext_ref, glu(vm_ref, gm_ref), glu(vp_ref, gp_ref), glu(vn_ref, gn_ref), first, last)
    tt = vm_ref.shape[0]

    def body(s, carry):
        r0 = pl.multiple_of(s * sub, sub)
        y = _conv_rows(ext_ref, w_ref, r0, sub, CONF_KERNEL) + b_ref[...]
        mu = jnp.mean(y, axis=-1, keepdims=True)
        yc = y - mu
        var = jnp.mean(yc * yc, axis=-1, keepdims=True)
        z = yc * lax.rsqrt(var + LN_EPS) * lnw_ref[...] + lnb_ref[...]
        o_ref[pl.ds(r0, sub), :] = _silu(z).astype(o_ref.dtype)
        return carry

    lax.fori_loop(0, tt // sub, body, 0)


def _conformer_conv(proj, col_val, col_gate, conv_w, conv_b, ln_w, ln_b, seq, tt=256, sub=16):
    rows = proj.shape[0]
    tc = D_MODEL
    vec = pl.BlockSpec((1, tc), lambda i, j: (0, 0))
    return pl.pallas_call(
        functools.partial(_conformer_kernel, tiles_per_seq=seq // tt, sub=sub),
        grid=(rows // tt, 1),
        in_specs=(_halo_specs(tt, tc, lambda j: col_val // tc, rows)
                  + _halo_specs(tt, tc, lambda j: col_gate // tc, rows)
                  + [pl.BlockSpec((CONF_KERNEL, tc), lambda i, j: (0, 0)), vec, vec, vec]),
        out_specs=pl.BlockSpec((tt, tc), lambda i, j: (i, 0)),
        out_shape=jax.ShapeDtypeStruct((rows, tc), BF16),
        scratch_shapes=[pltpu.VMEM((tt + 2 * HALO, tc), F32)],
        compiler_params=_cparams(("parallel", "arbitrary")),
    )(proj, proj, proj, proj, proj, proj, conv_w, conv_b.reshape(1, tc), ln_w.reshape(1, tc),
      ln_b.reshape(1, tc))


def _block_mask(n):
    ri = lax.broadcasted_iota(jnp.int32, (CHUNK, CHUNK), 0)
    ci = lax.broadcasted_iota(jnp.int32, (CHUNK, CHUNK), 1)
    return (ri // n) == (ci // n)


_NEUMANN_BLOCK = 16


def _delta_kernel(q_ref, k_ref, v_ref, gcol_ref, grow_ref, o_ref,
                  s_ref, l_ref, x_ref, p_ref, cx_ref, attn_ref, rhs_ref, kdt_ref, wq_ref, u_ref,
                  *, reverse):
    tb = q_ref.shape[0]
    n_chunks = tb // CHUNK
    groups = gcol_ref.shape[0]
    heads = [(g, a, r) for g in range(groups) for a in range(2) for r in range(2)]
    n_items = n_chunks * len(heads)

    def head_index(g, a, r):
        return (g * 2 + a) * 2 + r

    def item(c, g, a, r):
        return c * len(heads) + head_index(g, a, r)

    @pl.when(pl.program_id(2) == 0)
    def _():
        s_ref[...] = jnp.zeros_like(s_ref)

    ri = lax.broadcasted_iota(jnp.int32, (CHUNK, CHUNK), 0)
    ci = lax.broadcasted_iota(jnp.int32, (CHUNK, CHUNK), 1)
    incl = (ri <= ci) if reverse else (ri >= ci)
    strict = (ri < ci) if reverse else (ri > ci)
    eye = jnp.where(ri == ci, 1.0, 0.0).astype(F32)
    edge = 0 if reverse else CHUNK - 1

    def setup_chunk(c):
        for qk_head in range(2 * groups):
            g, a = divmod(qk_head, 2)
            rows = pl.ds(c * CHUNK, CHUNK)
            cols = gcol_ref[g, rows, :]
            rws = grow_ref[c, g * 8:(g + 1) * 8, :]
            qk_col = slice((g * 2 + a) * HEAD_DIM, (g * 2 + a + 1) * HEAD_DIM)
            q = q_ref[rows, qk_col]
            k = k_ref[rows, qk_col]
            kk = _dot_nt(k, k)
            qk = _dot_nt(q, k)
            kf = k.astype(F32)
            qf = q.astype(F32)
            kt = kf.T
            for r in range(2):
                j = a * 4 + r
                it = item(c, g, a, r)
                v_col = slice(head_index(g, a, r) * HEAD_DIM, (head_index(g, a, r) + 1) * HEAD_DIM)
                gcol = cols[:, j:j + 1]
                bcol = cols[:, j + 2:j + 3]
                grow = rws[j:j + 1, :]
                decay = jnp.exp(jnp.where(incl, gcol - grow, NEG_BIG))
                lmat = jnp.where(strict, kk * decay * bcol, 0.0)
                l0 = jnp.where(_block_mask(_NEUMANN_BLOCK), lmat, 0.0)
                l_ref[it] = lmat
                x_ref[it] = eye - l0
                p_ref[it] = l0.astype(BF16)
                attn_ref[it] = (qk * decay).astype(BF16)
                v = v_ref[rows, v_col].astype(F32)
                eg = jnp.exp(gcol)
                rhs_ref[it] = jnp.concatenate([v * bcol, kf * (bcol * eg)], axis=1).astype(BF16)
                gl = gcol[edge:edge + 1, :]
                kdt_ref[it] = (kt * jnp.exp(gl - grow)).astype(BF16)
                wq_ref[it, CHUNK:, :] = (qf * eg).astype(BF16)

    n_sq = _NEUMANN_BLOCK.bit_length() - 1

    def neumann(its):
        for it in its:
            p = p_ref[it]
            p_ref[it] = _dot(p, p).astype(BF16)
        for _ in range(1, n_sq - 1):
            for it in its:
                p = p_ref[it]
                px = _dot(jnp.concatenate([p, x_ref[it].astype(BF16)], axis=0), p)
                p_ref[it] = px[:CHUNK].astype(BF16)
                x_ref[it] = x_ref[it] + px[CHUNK:]
        for it in its:
            x = x_ref[it]
            x_ref[it] = x + _dot(x.astype(BF16), p_ref[it])

    per_chunk = len(heads)
    for c in range(n_chunks):
        setup_chunk(c)
        if c >= 1:
            neumann(range((c - 1) * per_chunk, c * per_chunk))
    neumann(range((n_chunks - 1) * per_chunk, n_items))

    n = _NEUMANN_BLOCK
    while n < CHUNK:
        band = _block_mask(2 * n) & jnp.logical_not(_block_mask(n))
        for it in range(n_items):
            c_band = jnp.where(band, l_ref[it], 0.0).astype(BF16)
            cx_ref[it] = _dot(c_band, x_ref[it].astype(BF16)).astype(BF16)
        for it in range(n_items):
            x = x_ref[it]
            x_ref[it] = x - _dot(x.astype(BF16), cx_ref[it])
        n *= 2

    for it in range(n_items):
        uw = _dot(x_ref[it].astype(BF16), rhs_ref[it])
        u_ref[it] = uw[:, :HEAD_DIM]
        wq_ref[it, :CHUNK, :] = uw[:, HEAD_DIM:].astype(BF16)

    for step in range(n_chunks):
        c = (n_chunks - 1 - step) if reverse else step
        rows = pl.ds(c * CHUNK, CHUNK)
        its = [item(c, *h) for h in heads]
        states = [s_ref[head_index(*h)] for h in heads]
        ws = [_dot(wq_ref[it], s.astype(BF16)) for it, s in zip(its, states)]
        v_new = [(u_ref[it] - w[:CHUNK]).astype(BF16) for it, w in zip(its, ws)]
        intra = [_dot(attn_ref[it], vn) for it, vn in zip(its, v_new)]
        ds = [_dot(kdt_ref[it], vn) for it, vn in zip(its, v_new)]
        edge_rows = [gcol_ref[g, c * CHUNK + edge:c * CHUNK + edge + 1, :] for g in range(groups)]
        for n_h, (g, a, r) in enumerate(heads):
            idx = head_index(g, a, r)
            j = a * 4 + r
            egl = jnp.exp(edge_rows[g][:, j:j + 1])
            s_ref[idx] = states[n_h] * egl + ds[n_h]
            o_ref[rows, idx * HEAD_DIM:(idx + 1) * HEAD_DIM] = (
                ws[n_h][CHUNK:] + intra[n_h]).astype(o_ref.dtype)


def _delta_scan(q, k, v, gcol3, grow3, direction, batch, seq, tb=512, groups=2):
    rows = q.shape[0]
    nblk = seq // tb
    reverse = direction == 1
    n_heads = 4 * groups
    n_items = n_heads * (tb // CHUNK)
    steps_h = N_QK_HEADS // (2 * groups)

    def blk(b, i):
        return b * nblk + ((nblk - 1 - i) if reverse else i)

    return pl.pallas_call(
        functools.partial(_delta_kernel, reverse=reverse),
        grid=(batch, steps_h, nblk),
        in_specs=[
            pl.BlockSpec((tb, 2 * groups * HEAD_DIM), lambda b, h, i: (blk(b, i), h)),
            pl.BlockSpec((tb, 2 * groups * HEAD_DIM), lambda b, h, i: (blk(b, i), h)),
            pl.BlockSpec((tb, n_heads * HEAD_DIM), lambda b, h, i: (blk(b, i), h)),
            pl.BlockSpec((groups, tb, 8), lambda b, h, i: (direction * steps_h + h, blk(b, i), 0)),
            pl.BlockSpec((tb // CHUNK, 8 * groups, CHUNK),
                         lambda b, h, i: (blk(b, i), direction * steps_h + h, 0)),
        ],
        out_specs=pl.BlockSpec((tb, n_heads * HEAD_DIM), lambda b, h, i: (blk(b, i), h)),
        out_shape=jax.ShapeDtypeStruct((rows, V_WIDTH), F32),
        scratch_shapes=[
            pltpu.VMEM((n_heads, HEAD_DIM, HEAD_DIM), F32),
            pltpu.VMEM((n_items, CHUNK, CHUNK), F32),
            pltpu.VMEM((n_items, CHUNK, CHUNK), F32),
            pltpu.VMEM((n_items, CHUNK, CHUNK), BF16),
            pltpu.VMEM((n_items, CHUNK, CHUNK), BF16),
            pltpu.VMEM((n_items, CHUNK, CHUNK), BF16),
            pltpu.VMEM((n_items, CHUNK, 2 * HEAD_DIM), BF16),
            pltpu.VMEM((n_items, HEAD_DIM, CHUNK), BF16),
            pltpu.VMEM((n_items, 2 * CHUNK, HEAD_DIM), BF16),
            pltpu.VMEM((n_items, CHUNK, HEAD_DIM), F32),
        ],
        compiler_params=_cparams(("parallel", "parallel", "arbitrary")),
    )(q, k, v, gcol3, grow3)


def _delta_out_kernel(of_ref, ob_ref, z_ref, nw_ref, w_ref, o_ref, a_ref):
    kstep = pl.program_id(1)
    tk = of_ref.shape[1]
    o = of_ref[...] + ob_ref[...]
    for hd in range(tk // HEAD_DIM):
        sl = slice(hd * HEAD_DIM, (hd + 1) * HEAD_DIM)
        oh = o[:, sl]
        ms = jnp.mean(oh * oh, axis=-1, keepdims=True)
        a_ref[:, sl] = ((oh * lax.rsqrt(ms + RMS_EPS) * nw_ref[...])
                        * _silu(z_ref[:, sl].astype(F32))).astype(BF16)
    part = _dot(a_ref[...], w_ref[...])

    @pl.when(kstep == 0)
    def _():
        o_ref[...] = part

    @pl.when(kstep != 0)
    def _():
        o_ref[...] += part


def _delta_out(o_f, o_b, proj, col_z, norm_w, w, tm=512, tk=1024):
    m = o_f.shape[0]
    n = w.shape[1]
    zb0 = col_z // tk
    return pl.pallas_call(
        _delta_out_kernel,
        grid=(m // tm, V_WIDTH // tk),
        in_specs=[
            pl.BlockSpec((tm, tk), lambda i, kk: (i, kk)),
            pl.BlockSpec((tm, tk), lambda i, kk: (i, kk)),
            pl.BlockSpec((tm, tk), lambda i, kk: (i, zb0 + kk)),
            pl.BlockSpec((1, HEAD_DIM), lambda i, kk: (0, 0)),
            pl.BlockSpec((tk, n), lambda i, kk: (kk, 0)),
        ],
        out_specs=pl.BlockSpec((tm, n), lambda i, kk: (i, 0)),
        out_shape=jax.ShapeDtypeStruct((m, n), F32),
        scratch_shapes=[pltpu.VMEM((tm, tk), BF16)],
        compiler_params=_cparams(("parallel", "arbitrary")),
    )(o_f, o_b, proj, norm_w.reshape(1, HEAD_DIM), w)


def _mm_kernel(a_ref, w_ref, o_ref):
    o_ref[...] = _dot(a_ref[...], w_ref[...]).astype(o_ref.dtype)


def _matmul(a, w, out_dtype, tm, tn):
    m, k = a.shape
    n = w.shape[1]
    return pl.pallas_call(
        _mm_kernel,
        grid=(m // tm, n // tn),
        in_specs=[pl.BlockSpec((tm, k), lambda i, j: (i, 0)),
                  pl.BlockSpec((k, tn), lambda i, j: (0, j))],
        out_specs=pl.BlockSpec((tm, tn), lambda i, j: (i, j)),
        out_shape=jax.ShapeDtypeStruct((m, n), out_dtype),
        compiler_params=_cparams(("parallel", "arbitrary")),
    )(a, w)


def _mix_kernel(ya_ref, yb_ref, ga_ref, gb_ref, x_ref, w_ref, g_ref, o_ref):
    merged = (_sigmoid(ga_ref[...].astype(F32)) * ya_ref[...]
              + _sigmoid(gb_ref[...].astype(F32)) * yb_ref[...]).astype(BF16)
    y = _dot(merged, w_ref[...])
    ms = jnp.mean(y * y, axis=-1, keepdims=True)
    o_ref[...] = x_ref[...] + y * lax.rsqrt(ms + RMS_EPS) * g_ref[...]


def _mix_out(y_a, y_b, proj, col_ga, col_gb, x, w, gain, tm=256):
    m, d = x.shape
    row = lambda i: (i, 0)
    return pl.pallas_call(
        _mix_kernel,
        grid=(m // tm,),
        in_specs=[
            pl.BlockSpec((tm, d), row),
            pl.BlockSpec((tm, d), row),
            pl.BlockSpec((tm, d), lambda i: (i, col_ga // d)),
            pl.BlockSpec((tm, d), lambda i: (i, col_gb // d)),
            pl.BlockSpec((tm, d), row),
            pl.BlockSpec((d, d), lambda i: (0, 0)),
            pl.BlockSpec((1, d), lambda i: (0, 0)),
        ],
        out_specs=pl.BlockSpec((tm, d), row),
        out_shape=jax.ShapeDtypeStruct((m, d), F32),
        compiler_params=_cparams(("parallel",)),
    )(y_a, y_b, proj, proj, x, w, gain.reshape(1, d))


def _ffn_down_kernel(gm_ref, gp_ref, gn_ref, vm_ref, vp_ref, vn_ref, cwg_ref, cwv_ref, w_ref,
                     x_ref, gain_ref, o_ref, extg_ref, extv_ref, act_ref, acc_ref,
                     *, tiles_per_seq, sub, mm_rows):
    kstep = pl.program_id(1)
    first, last = _seq_edges(tiles_per_seq)
    _fill_ext(extg_ref, gm_ref[...].astype(F32), gp_ref[...].astype(F32), gn_ref[...].astype(F32),
              first, last)
    _fill_ext(extv_ref, vm_ref[...].astype(F32), vp_ref[...].astype(F32), vn_ref[...].astype(F32),
              first, last)
    tm = gm_ref.shape[0]

    @pl.when(kstep == 0)
    def _():
        acc_ref[...] = jnp.zeros_like(acc_ref)

    for m0 in range(0, tm, mm_rows):
        for r0 in range(m0, m0 + mm_rows, sub):
            for j in range(act_ref.shape[1] // LANES):
                gt = _conv_col(extg_ref, cwg_ref, r0, sub, FFN_CONV, j)
                val = _conv_col(extv_ref, cwv_ref, r0, sub, FFN_CONV, j)
                act_ref[r0:r0 + sub, j * LANES:(j + 1) * LANES] = (_silu(gt) * val).astype(BF16)
        acc_ref[m0:m0 + mm_rows, :] += _dot(act_ref[m0:m0 + mm_rows, :], w_ref[...])

    @pl.when(kstep == pl.num_programs(1) - 1)
    def _():
        f = acc_ref[...]
        ms = jnp.mean(f * f, axis=-1, keepdims=True)
        o_ref[...] = x_ref[...] + f * lax.rsqrt(ms + RMS_EPS) * gain_ref[...]


def _ffn_down(up, conv_w, w, x, gain, seq, tm=512, tk=512, sub=64, mm_rows=256):
    m, d = x.shape
    nk = D_FF // tk
    return pl.pallas_call(
        functools.partial(_ffn_down_kernel, tiles_per_seq=seq // tm, sub=sub, mm_rows=mm_rows),
        grid=(m // tm, nk),
        in_specs=(_halo_specs(tm, tk, lambda kk: kk, m) + _halo_specs(tm, tk, lambda kk: nk + kk, m)
                  + [pl.BlockSpec((FFN_CONV, tk), lambda i, kk: (0, kk)),
                     pl.BlockSpec((FFN_CONV, tk), lambda i, kk: (0, nk + kk)),
                     pl.BlockSpec((tk, d), lambda i, kk: (kk, 0)),
                     pl.BlockSpec((tm, d), lambda i, kk: (i, 0)),
                     pl.BlockSpec((1, d), lambda i, kk: (0, 0))]),
        out_specs=pl.BlockSpec((tm, d), lambda i, kk: (i, 0)),
        out_shape=jax.ShapeDtypeStruct((m, d), F32),
        scratch_shapes=[pltpu.VMEM((tm + 2 * HALO, tk), F32), pltpu.VMEM((tm + 2 * HALO, tk), F32),
                        pltpu.VMEM((tm, tk), BF16), pltpu.VMEM((tm, d), F32)],
        compiler_params=_cparams(("parallel", "arbitrary")),
    )(up, up, up, up, up, up, conv_w, conv_w, w, x, gain.reshape(1, d))


_COL_Q = 0
_COL_K = QK_WIDTH
_COL_V = 2 * QK_WIDTH
_COL_Z = 2 * QK_WIDTH + V_WIDTH
_COL_GLU_V = 2 * QK_WIDTH + 2 * V_WIDTH
_COL_GLU_G = _COL_GLU_V + D_MODEL
_COL_GA = _COL_GLU_G + D_MODEL
_COL_GB = _COL_GA + D_MODEL
_AB_COL0 = 2 * QK_WIDTH + 2 * V_WIDTH
_AB_WIDTH = 2 * N_DIR * N_V_HEADS


def _table_permutation():
    src = np.zeros(128, np.int32)
    head = np.zeros(128, np.int32)
    for n in range(128):
        d, hp, a, kind, r = n // 64, (n % 64) // 8, (n % 8) // 4, (n % 4) // 2, n % 2
        vh = 4 * hp + 2 * a + r
        src[n] = kind * (N_DIR * N_V_HEADS) + d * N_V_HEADS + vh
        head[n] = d * N_V_HEADS + vh
    return src, head


def _encoder_layer(x3, p):
    batch, seq, d = x3.shape
    rows = batch * seq
    x = x3.reshape(rows, d)

    w_in = p["w_in"]
    w_main = jnp.concatenate([w_in[:, :_AB_COL0], w_in[:, _AB_COL0 + _AB_WIDTH:]], axis=1).astype(BF16)
    src, head = _table_permutation()
    w_ab = w_in[:, _AB_COL0:_AB_COL0 + _AB_WIDTH][:, src].astype(BF16)
    a_log_p = p["a_log"].reshape(-1)[head]
    dtb_p = p["dt_bias"].reshape(-1)[head]

    proj = _norm_matmul(x, p["mix_norm_pre"], w_main, BF16, tm=1024, tn=1024)
    gcol, grow3 = _gate_tables(x, p["mix_norm_pre"], w_ab, a_log_p, dtb_p, tm=512)
    gcol3 = gcol.reshape(rows, 16, 8).transpose(1, 0, 2)

    cw = p["short_conv_w"]
    q = _short_conv(proj, cw, _COL_Q, QK_WIDTH, "q", seq)
    k = _short_conv(proj, cw, _COL_K, QK_WIDTH, "k", seq)
    v = _short_conv(proj, cw, _COL_V, V_WIDTH, "v", seq)
    o_f = _delta_scan(q, k, v, gcol3, grow3, 0, batch, seq)
    o_b = _delta_scan(q, k, v, gcol3, grow3, 1, batch, seq)
    y_a = _delta_out(o_f, o_b, proj, _COL_Z, p["delta_norm_w"], p["w_delta_out"].astype(BF16))

    hcv = _conformer_conv(proj, _COL_GLU_V, _COL_GLU_G, p["conf_conv_w"], p["conf_conv_b"],
                          p["conf_ln_w"], p["conf_ln_b"], seq)
    y_b = _matmul(hcv, p["w_conf_out"].astype(BF16), F32, tm=1024, tn=1024)

    x1 = _mix_out(y_a, y_b, proj, _COL_GA, _COL_GB, x, p["w_mix_out"].astype(BF16),
                  p["mix_norm_post"])

    up = _norm_matmul(x1, p["ffn_norm_pre"], p["w_up"].astype(BF16), BF16, tm=1024, tn=1024)
    out = _ffn_down(up, p["ffn_conv_w"], p["w_down"].astype(BF16), x1, p["ffn_norm_post"], seq)
    return out.reshape(batch, seq, d)


_PARAM_NAMES = ("mix_norm_pre", "w_in", "short_conv_w", "a_log", "dt_bias", "delta_norm_w",
                "w_delta_out", "conf_conv_w", "conf_conv_b", "conf_ln_w", "conf_ln_b", "w_conf_out",
                "w_mix_out", "mix_norm_post", "ffn_norm_pre", "w_up", "ffn_conv_w", "w_down",
                "ffn_norm_post")


def kernel(x_prompt, x_sample, mix_norm_pre, w_in, short_conv_w, a_log, dt_bias, delta_norm_w,
           w_delta_out, conf_conv_w, conf_conv_b, conf_ln_w, conf_ln_b, w_conf_out, w_mix_out,
           mix_norm_post, ffn_norm_pre, w_up, ffn_conv_w, w_down, ffn_norm_post):
    stacked = (mix_norm_pre, w_in, short_conv_w, a_log, dt_bias, delta_norm_w, w_delta_out,
               conf_conv_w, conf_conv_b, conf_ln_w, conf_ln_b, w_conf_out, w_mix_out, mix_norm_post,
               ffn_norm_pre, w_up, ffn_conv_w, w_down, ffn_norm_post)
    depth = w_in.shape[0]
    outs = []
    for x in (x_prompt, x_sample):
        for layer in range(depth):
            x = _encoder_layer(x, {n: t[layer] for n, t in zip(_PARAM_NAMES, stacked)})
        outs.append(x)
    return tuple(outs)
```
